```python
import math
import jax, jax.numpy as jnp
from jax import lax
import numpy as np

D_MODEL = 1024
BATCH = 16
SEQ = 2048
DEPTH = 4

GRID_W = 64
CTX_LEN = 256

ATT_HEADS = 8
ATT_HEAD_DIM = 64
ATT_QK_DIM = ATT_HEADS * 2 * ATT_HEAD_DIM
ATT_V_DIM = ATT_HEADS * 2 * ATT_HEAD_DIM
Q_BLOCK = 128
ROPE_BASE = 10000.0
ROPE_PAIRS = ATT_HEAD_DIM // 4

SSD_D_INNER = 2 * D_MODEL
SSD_HEAD_DIM = 64
SSD_HEADS = SSD_D_INNER // SSD_HEAD_DIM
SSD_GROUPS = 8
SSD_STATE = 128
SSD_CONV = 5
SSD_CHUNK = 128
SSD_CONV_DIM = SSD_D_INNER + 2 * SSD_GROUPS * SSD_STATE

D_FF = 4 * D_MODEL

N_BRANCH = 2
N_MOD = 6
IN_DIM = 2 * ATT_QK_DIM + ATT_V_DIM + SSD_D_INNER + SSD_CONV_DIM + 2 * SSD_HEADS + N_BRANCH * D_MODEL

kernel_name = 'hybrid_diffattn_ssd_dit_trunk'


def rmsnorm(x, g, eps=1e-6):
    xf = x.astype(jnp.float32)
    y = xf * lax.rsqrt(jnp.mean(xf * xf, axis=-1, keepdims=True) + eps)
    return (y * g.astype(jnp.float32)).astype(x.dtype)


def modulate(x, g, shift, scale):
    return rmsnorm(x, g) * (1 + scale) + shift


def in_proj_split(t):
    sizes = (ATT_QK_DIM, ATT_QK_DIM, ATT_V_DIM, SSD_D_INNER, SSD_CONV_DIM,
             SSD_HEADS, SSD_HEADS, D_MODEL, D_MODEL)
    idx, acc = [], 0
    for s in sizes[:-1]:
        acc += s
        idx.append(acc)
    return jnp.split(t, idx, axis=-1)


def axial_rope(n_tok):
    rows = n_tok // GRID_W
    row = jnp.broadcast_to(jnp.arange(rows)[:, None], (rows, GRID_W)).reshape(-1).astype(jnp.float32)
    col = jnp.broadcast_to(jnp.arange(GRID_W)[None, :], (rows, GRID_W)).reshape(-1).astype(jnp.float32)
    inv = jnp.float32(ROPE_BASE) ** (-jnp.arange(ROPE_PAIRS, dtype=jnp.float32) / ROPE_PAIRS)
    ang = jnp.concatenate([row[:, None] * inv, col[:, None] * inv], axis=-1)
    return jnp.cos(ang), jnp.sin(ang)


def apply_rope(x, cos, sin):
    half = x.shape[-1] // 2
    x1 = x[..., :half].astype(jnp.float32)
    x2 = x[..., half:].astype(jnp.float32)
    c = cos[:, None, None, :]
    s = sin[:, None, None, :]
    return jnp.concatenate([x1 * c - x2 * s, x1 * s + x2 * c], axis=-1).astype(x.dtype)


def diff_attend(q, k, v, lam):
    s = jnp.einsum('bqhcd,bkhcd->bchqk', q, k).astype(jnp.float32) * (ATT_HEAD_DIM ** -0.5)
    p = jax.nn.softmax(s, axis=-1)
    a = p[:, 0] - lam * p[:, 1]
    return jnp.einsum('bhqk,bkhe->bqhe', a.astype(v.dtype), v)


def diff_attention(q_l, k_l, v_l, q_c, k_c, v_c, lam, lam_init, subln_g, with_ctx_out):
    b, n, h = q_l.shape[:3]
    k_all = jnp.concatenate([k_l, k_c], axis=1)
    v_all = jnp.concatenate([v_l, v_c], axis=1)
    nb = n // Q_BLOCK
    qb = jnp.moveaxis(q_l.reshape(b, nb, Q_BLOCK, h, 2, ATT_HEAD_DIM), 1, 0)
    o_l = lax.map(lambda qq: diff_attend(qq, k_all, v_all, lam), qb)
    o_l = jnp.moveaxis(o_l, 0, 1).reshape(b, n, h, 2 * ATT_HEAD_DIM)

    def post(o):
        return (rmsnorm(o, subln_g) * (1.0 - lam_init)).reshape(o.shape[0], o.shape[1], ATT_V_DIM)

    o_c = post(diff_attend(q_c, k_c, v_c, lam)) if with_ctx_out else None
    return post(o_l), o_c


def dwconv_silu(x, w, bias):
    kw = w.shape[0]
    y = lax.conv_general_dilated(x, w.astype(x.dtype)[:, None, :], window_strides=(1,),
                                 padding=[(kw // 2, kw // 2)],
                                 dimension_numbers=('NWC', 'WIO', 'NWC'),
                                 feature_group_count=x.shape[-1])
    return jax.nn.silu(y + bias)


def ssd_chunked(x, dt, a, bm, cm, h0, need_y):
    f32 = jnp.float32
    out_dtype = x.dtype
    b, n, h, p = x.shape
    g, ns = bm.shape[2], bm.shape[3]
    r = h // g
    nc = n // SSD_CHUNK
    dt = dt.astype(f32)
    xc = (x.astype(f32) * dt[..., None]).reshape(b, nc, SSD_CHUNK, g, r, p)
    cs = jnp.cumsum((dt * a.astype(f32)).reshape(b, nc, SSD_CHUNK, g, r), axis=2)
    bc = bm.astype(f32).reshape(b, nc, SSD_CHUNK, g, ns)
    cc = cm.astype(f32).reshape(b, nc, SSD_CHUNK, g, ns)
    decay_to_end = jnp.exp(cs[:, :, -1:] - cs)
    chunk_states = jnp.einsum('bclgn,bclgr,bclgrp->bcgrpn', bc, decay_to_end, xc)
    chunk_decay = jnp.exp(cs[:, :, -1])

    def step(hc, inp):
        st, dec = inp
        return hc * dec[..., None, None] + st, hc

    h_last, h_in = lax.scan(step, h0.astype(f32),
                            (jnp.moveaxis(chunk_states, 1, 0), jnp.moveaxis(chunk_decay, 1, 0)))
    if not need_y:
        return None, h_last
    h_in = jnp.moveaxis(h_in, 0, 1)
    seg = cs[:, :, :, None] - cs[:, :, None, :]
    lower = jnp.tril(jnp.ones((SSD_CHUNK, SSD_CHUNK), dtype=bool))[:, :, None, None]
    lmat = jnp.exp(jnp.where(lower, seg, -jnp.inf))
    cb = jnp.einsum('bclgn,bcsgn->bclsg', cc, bc)
    y = (jnp.einsum('bclsg,bclsgr,bcsgrp->bclgrp', cb, lmat, xc)
         + jnp.einsum('bclgn,bcgrpn,bclgr->bclgrp', cc, h_in, jnp.exp(cs)))
    return y.reshape(b, n, h, p).astype(out_dtype), h_last


def ssd_inputs(xbc_raw, dtf_raw, dtb_raw, conv_w, conv_b, dt_bias_f, dt_bias_b):
    xbc = dwconv_silu(xbc_raw, conv_w, conv_b)
    xs, bm, cm = jnp.split(xbc, [SSD_D_INNER, SSD_D_INNER + SSD_GROUPS * SSD_STATE], axis=-1)
    b, n = xs.shape[:2]
    xs = xs.reshape(b, n, SSD_HEADS, SSD_HEAD_DIM)
    bm = bm.reshape(b, n, SSD_GROUPS, SSD_STATE)
    cm = cm.reshape(b, n, SSD_GROUPS, SSD_STATE)
    dt_f = jax.nn.softplus((dtf_raw + dt_bias_f).astype(jnp.float32))
    dt_b = jax.nn.softplus((dtb_raw + dt_bias_b).astype(jnp.float32))
    return xs, bm, cm, dt_f, dt_b


def ssd_output(yf, yb, xs, z, ssd_d, ssd_norm_g, w_ssd_o):
    b, n = xs.shape[:2]
    y = (yf + yb + ssd_d[:, None] * xs).reshape(b, n, SSD_D_INNER)
    return rmsnorm(y * jax.nn.silu(z), ssd_norm_g) @ w_ssd_o


def hybrid_mixer(h_l, h_c, cos, sin, w_in, conv_w, conv_b, dt_bias_f, dt_bias_b, a_log_f, a_log_b,
                 ssd_d, ssd_norm_g, lam, lam_init, subln_g, w_attn_o, w_ssd_o, w_out, with_ctx_out):
    ql, kl, vl, zl, xbcl, dtfl, dtbl, gal, gsl = in_proj_split(h_l @ w_in)
    qc, kc, vc, zc, xbcc, dtfc, dtbc, gac, gsc = in_proj_split(h_c @ w_in)

    def qk(t):
        return t.reshape(t.shape[0], t.shape[1], ATT_HEADS, 2, ATT_HEAD_DIM)

    def vv(t):
        return t.reshape(t.shape[0], t.shape[1], ATT_HEADS, 2 * ATT_HEAD_DIM)

    att_l, att_c = diff_attention(apply_rope(qk(ql), cos, sin), apply_rope(qk(kl), cos, sin), vv(vl),
                                  qk(qc), qk(kc), vv(vc), lam, lam_init, subln_g, with_ctx_out)

    a_f = -jnp.exp(a_log_f.astype(jnp.float32))
    a_b = -jnp.exp(a_log_b.astype(jnp.float32))
    rev = lambda t: jnp.flip(t, axis=1)
    xs_c, bm_c, cm_c, dtf_c, dtb_c = ssd_inputs(xbcc, dtfc, dtbc, conv_w, conv_b, dt_bias_f, dt_bias_b)
    xs_l, bm_l, cm_l, dtf_l, dtb_l = ssd_inputs(xbcl, dtfl, dtbl, conv_w, conv_b, dt_bias_f, dt_bias_b)
    b = h_l.shape[0]
    h0 = jnp.zeros((b, SSD_GROUPS, SSD_HEADS // SSD_GROUPS, SSD_HEAD_DIM, SSD_STATE), jnp.float32)
    yc_f, st_f = ssd_chunked(xs_c, dtf_c, a_f, bm_c, cm_c, h0, with_ctx_out)
    yc_b, st_b = ssd_chunked(rev(xs_c), rev(dtb_c), a_b, rev(bm_c), rev(cm_c), h0, with_ctx_out)
    yl_f, _ = ssd_chunked(xs_l, dtf_l, a_f, bm_l, cm_l, st_f, True)
    yl_b, _ = ssd_chunked(rev(xs_l), rev(dtb_l), a_b, rev(bm_l), rev(cm_l), st_b, True)
    ssd_l = ssd_output(yl_f, rev(yl_b), xs_l, zl, ssd_d, ssd_norm_g, w_ssd_o)

    def merge(att, ssd, ga, gs):
        return (jax.nn.sigmoid(ga) * (att @ w_attn_o) + jax.nn.sigmoid(gs) * ssd) @ w_out

    o_l = merge(att_l, ssd_l, gal, gsl)
    o_c = None
    if with_ctx_out:
        ssd_c = ssd_output(yc_f, rev(yc_b), xs_c, zc, ssd_d, ssd_norm_g, w_ssd_o)
        o_c = merge(att_c, ssd_c, gac, gsc)
    return o_l, o_c


def sq_relu_mlp(h, w1, w2):
    return jnp.square(jax.nn.relu(h @ w1)) @ w2


def setup_inputs(seed: int = 0) -> dict:
    key = jax.random.key(seed)
    ks = jax.random.split(key, 32)
    f32 = jnp.float32

    def nrm(k, shape, scale):
        return jax.random.normal(k, shape, f32) * scale

    def gain(k, shape):
        return 1.0 + 0.05 * jax.random.normal(k, shape, f32)

    dt = jnp.exp(jax.random.uniform(ks[10], (DEPTH, 2, SSD_HEADS), f32,
                                    math.log(1e-3), math.log(1e-1)))
    dt_bias = dt + jnp.log(-jnp.expm1(-dt))
    a_log = jnp.log(jax.random.uniform(ks[11], (DEPTH, 2, SSD_HEADS), f32, 1.0, 16.0))
    return {
        'x': nrm(ks[0], (BATCH, SEQ, D_MODEL), 1.0),
        'c': nrm(ks[1], (BATCH, D_MODEL), 1.0),
        'ctx': nrm(ks[2], (BATCH, CTX_LEN, D_MODEL), 1.0),
        'c_ctx': nrm(ks[3], (D_MODEL,), 1.0),
        'ada_w': nrm(ks[4], (DEPTH, D_MODEL, N_MOD * D_MODEL), 0.5 * D_MODEL ** -0.5),
        'ada_b': nrm(ks[5], (DEPTH, N_MOD * D_MODEL), 0.02),
        'norm_mix_g': gain(ks[6], (DEPTH, D_MODEL)),
        'w_in': nrm(ks[7], (DEPTH, D_MODEL, IN_DIM), D_MODEL ** -0.5),
        'conv_w': nrm(ks[8], (DEPTH, SSD_CONV, SSD_CONV_DIM), SSD_CONV ** -0.5),
        'conv_b': nrm(ks[9], (DEPTH, SSD_CONV_DIM), 0.02),
        'dt_bias_f': dt_bias[:, 0],
        'dt_bias_b': dt_bias[:, 1],
        'a_log_f': a_log[:, 0],
        'a_log_b': a_log[:, 1],
        'ssd_d': gain(ks[12], (DEPTH, SSD_HEADS)),
        'ssd_norm_g': gain(ks[13], (DEPTH, SSD_D_INNER)),
        'lambda_q1': nrm(ks[14], (DEPTH, ATT_HEAD_DIM), 0.1),
        'lambda_k1': nrm(ks[15], (DEPTH, ATT_HEAD_DIM), 0.1),
        'lambda_q2': nrm(ks[16], (DEPTH, ATT_HEAD_DIM), 0.1),
        'lambda_k2': nrm(ks[17], (DEPTH, ATT_HEAD_DIM), 0.1),
        'attn_subln_g': gain(ks[18], (DEPTH, 2 * ATT_HEAD_DIM)),
        'w_attn_o': nrm(ks[19], (DEPTH, ATT_V_DIM, D_MODEL), ATT_V_DIM ** -0.5),
        'w_ssd_o': nrm(ks[20], (DEPTH, SSD_D_INNER, D_MODEL), SSD_D_INNER ** -0.5),
        'w_out': nrm(ks[21], (DEPTH, D_MODEL, D_MODEL), D_MODEL ** -0.5),
        'norm_mlp_g': gain(ks[22], (DEPTH, D_MODEL)),
        'w_mlp1': nrm(ks[23], (DEPTH, D_MODEL, D_FF), D_MODEL ** -0.5),
        'w_mlp2': nrm(ks[24], (DEPTH, D_FF, D_MODEL), D_FF ** -0.5),
        'final_norm_g': gain(ks[25], (D_MODEL,)),
    }


def reference(x, c, ctx, c_ctx, ada_w, ada_b, norm_mix_g, w_in, conv_w, conv_b, dt_bias_f, dt_bias_b,
              a_log_f, a_log_b, ssd_d, ssd_norm_g, lambda_q1, lambda_k1, lambda_q2, lambda_k2,
              attn_subln_g, w_attn_o, w_ssd_o, w_out, norm_mlp_g, w_mlp1, w_mlp2, final_norm_g):
    n_lat = x.shape[1]
    cos, sin = axial_rope(n_lat)
    c_act = jax.nn.silu(c)
    cc_act = jax.nn.silu(c_ctx)
    xl, xc = x, ctx
    for l in range(DEPTH):
        last = l == DEPTH - 1
        mod_l = [m[:, None, :] for m in jnp.split(c_act @ ada_w[l] + ada_b[l], N_MOD, axis=-1)]
        mod_c = jnp.split(cc_act @ ada_w[l] + ada_b[l], N_MOD, axis=-1)
        lam_init = 0.8 - 0.6 * math.exp(-0.3 * l)
        lam = (jnp.exp(jnp.sum(lambda_q1[l].astype(jnp.float32) * lambda_k1[l].astype(jnp.float32)))
               - jnp.exp(jnp.sum(lambda_q2[l].astype(jnp.float32) * lambda_k2[l].astype(jnp.float32)))
               + lam_init)
        h_l = modulate(xl, norm_mix_g[l], mod_l[0], mod_l[1])
        h_c = modulate(xc, norm_mix_g[l], mod_c[0], mod_c[1])
        o_l, o_c = hybrid_mixer(h_l, h_c, cos, sin, w_in[l], conv_w[l], conv_b[l], dt_bias_f[l], dt_bias_b[l],
                                a_log_f[l], a_log_b[l], ssd_d[l], ssd_norm_g[l], lam, lam_init,
                                attn_subln_g[l], w_attn_o[l], w_ssd_o[l], w_out[l], not last)
        xl = xl + mod_l[2] * o_l
        xl = xl + mod_l[5] * sq_relu_mlp(modulate(xl, norm_mlp_g[l], mod_l[3], mod_l[4]), w_mlp1[l], w_mlp2[l])
        if not last:
            xc = xc + mod_c[2] * o_c
            xc = xc + mod_c[5] * sq_relu_mlp(modulate(xc, norm_mlp_g[l], mod_c[3], mod_c[4]), w_mlp1[l], w_mlp2[l])
    return rmsnorm(xl, final_norm_g)
```

```python
import functools
import math

import jax
import jax.numpy as jnp
from jax import lax
from jax.experimental import pallas as pl
from jax.experimental.pallas import tpu as pltpu

F32 = jnp.float32
BF16 = jnp.bfloat16

D_MODEL = 1024
GRID_W = 64
ATT_HEADS = 8
ATT_HEAD_DIM = 64
ATT_BLOCK = 2 * ATT_HEAD_DIM
ROPE_BASE = 10000.0
ROPE_PAIRS = ATT_HEAD_DIM // 4
SSD_D_INNER = 2 * D_MODEL
SSD_HEAD_DIM = 64
SSD_HEADS = SSD_D_INNER // SSD_HEAD_DIM
SSD_GROUPS = 8
SSD_GROUP_HEADS = SSD_HEADS // SSD_GROUPS
SSD_GROUP_COLS = SSD_GROUP_HEADS * SSD_HEAD_DIM
SSD_STATE = 128
SSD_CONV = 5
SSD_CHUNK = 128
SSD_BC = 2 * SSD_GROUPS * SSD_STATE
D_FF = 4 * D_MODEL
N_MOD = 6
EPS = 1e-6
CONV_HALO = 8
MOD_ROWS_PAD = 8
VMEM_LIMIT = 56 * 1024 * 1024


def _params(sem, vmem=None):
    return pltpu.CompilerParams(dimension_semantics=sem, vmem_limit_bytes=vmem)


def _pick(n, candidates):
    for c in candidates:
        if n % c == 0:
            return c
    raise ValueError(f"no tile in {candidates} divides {n}")


def _silu(v):
    return v * (1.0 / (1.0 + jnp.exp(-v)))


def _sigmoid(v):
    return 1.0 / (1.0 + jnp.exp(-v))


def _rms(v, g):
    return v * lax.rsqrt(jnp.mean(v * v, axis=-1, keepdims=True) + EPS) * g


def _is_latent(tm, tiles_per_batch, n_lat):
    off = (pl.program_id(0) % tiles_per_batch) * tm
    rows = lax.broadcasted_iota(jnp.int32, (tm, 1), 0)
    return rows < (n_lat - off)


def _mod(ml_ref, mc_ref, k, is_lat):
    return jnp.where(is_lat, ml_ref[k:k + 1, :], mc_ref[k:k + 1, :])


def _mod_kernel(c_ref, w_ref, b_ref, o_ref):
    a = _silu(c_ref[...]).astype(BF16)
    o_ref[...] = jnp.dot(a, w_ref[...].astype(BF16), preferred_element_type=F32) + b_ref[...]


def _mod_call(c_all, ada_w, ada_b):
    depth, d, nm = ada_w.shape
    rows = c_all.shape[0]
    tn = 1024
    return pl.pallas_call(
        _mod_kernel,
        grid=(depth, nm // tn),
        in_specs=[pl.BlockSpec((rows, d), lambda l, j: (0, 0)),
                  pl.BlockSpec((None, d, tn), lambda l, j: (l, 0, j)),
                  pl.BlockSpec((None, 1, tn), lambda l, j: (l, 0, j))],
        out_specs=pl.BlockSpec((None, rows, tn), lambda l, j: (l, 0, j)),
        out_shape=jax.ShapeDtypeStruct((depth, rows, nm), F32),
        compiler_params=_params(("arbitrary", "arbitrary")),
        name="adaln_mod",
    )(c_all, ada_w, ada_b.reshape(depth, 1, nm))


def _norm_mod_kernel(x_ref, ml_ref, mc_ref, g_ref, o_ref, *, tm, tpb, n_lat):
    is_lat = _is_latent(tm, tpb, n_lat)
    h = _rms(x_ref[...], g_ref[...]) * (1.0 + _mod(ml_ref, mc_ref, 1, is_lat)) + _mod(ml_ref, mc_ref, 0, is_lat)
    o_ref[...] = h.astype(o_ref.dtype)


def _norm_mod_call(x, mods, g, n_tok, n_lat, batch):
    t, d = x.shape
    tm = _pick(n_tok, (768, 384, 128))
    tpb = n_tok // tm
    return pl.pallas_call(
        functools.partial(_norm_mod_kernel, tm=tm, tpb=tpb, n_lat=n_lat),
        grid=(t // tm,),
        in_specs=[pl.BlockSpec((tm, d), lambda i: (i, 0)),
                  pl.BlockSpec((None, N_MOD, d), lambda i: (i // tpb, 0, 0)),
                  pl.BlockSpec((None, N_MOD, d), lambda i: (batch, 0, 0)),
                  pl.BlockSpec((1, d), lambda i: (0, 0))],
        out_specs=pl.BlockSpec((tm, d), lambda i: (i, 0)),
        out_shape=jax.ShapeDtypeStruct((t, d), BF16),
        compiler_params=_params(("parallel",)),
        name="norm_modulate",
    )(x, mods, mods, g)


def _proj_kernel(a_ref, w_ref, o_ref):
    o_ref[...] = jnp.dot(a_ref[...], w_ref[...], preferred_element_type=F32).astype(o_ref.dtype)


def _proj_rope_kernel(a_ref, w_ref, cos_ref, sin_ref, o_ref, *, tn):
    acc = jnp.dot(a_ref[...], w_ref[...], preferred_element_type=F32)
    cos = cos_ref[...]
    sin = sin_ref[...]
    lane = lax.broadcasted_iota(jnp.int32, cos.shape, 1)
    first_half = (lane % ATT_HEAD_DIM) < (ATT_HEAD_DIM // 2)
    for j in range(tn // ATT_BLOCK):
        blk = acc[:, j * ATT_BLOCK:(j + 1) * ATT_BLOCK]
        partner = jnp.where(first_half,
                            pltpu.roll(blk, ATT_BLOCK - ATT_HEAD_DIM // 2, 1),
                            pltpu.roll(blk, ATT_HEAD_DIM // 2, 1))
        o_ref[:, j * ATT_BLOCK:(j + 1) * ATT_BLOCK] = (blk * cos + partner * sin).astype(o_ref.dtype)


def _proj_call(h, w, out_dtype, n_tok, rope=None):
    t, k = h.shape
    n = w.shape[1]
    tm = _pick(n_tok, (1152, 384, 128))
    tpb = n_tok // tm
    tn = _pick(n, (1024, 512, 256, 128))
    in_specs = [pl.BlockSpec((tm, k), lambda i, j: (i, 0)),
                pl.BlockSpec((k, tn), lambda i, j: (0, j))]
    args = [h, w]
    if rope is None:
        body = _proj_kernel
    else:
        body = functools.partial(_proj_rope_kernel, tn=tn)
        in_specs += [pl.BlockSpec((tm, ATT_BLOCK), lambda i, j: (i % tpb, 0)),
                     pl.BlockSpec((tm, ATT_BLOCK), lambda i, j: (i % tpb, 0))]
        args += list(rope)
    return pl.pallas_call(
        body,
        grid=(t // tm, n // tn),
        in_specs=in_specs,
        out_specs=pl.BlockSpec((tm, tn), lambda i, j: (i, j)),
        out_shape=jax.ShapeDtypeStruct((t, n), out_dtype),
        compiler_params=_params(("parallel", "arbitrary"), VMEM_LIMIT),
        name="in_proj_rope" if rope is not None else "in_proj",
    )(*args)


def _dt_kernel(w_ref, h_ref, b_ref, o_ref):
    raw = lax.dot_general(w_ref[...], h_ref[...], (((1,), (1,)), ((), ())), preferred_element_type=F32)
    v = raw + b_ref[...]
    o_ref[...] = jnp.maximum(v, 0.0) + jnp.log1p(jnp.exp(-jnp.abs(v)))


def _dt_call(h, w_dt_t, dt_bias, n_tok, batch):
    t, k = h.shape
    nh = w_dt_t.shape[0]
    tm = _pick(n_tok, (1152, 384, 128))
    tpb = n_tok // tm
    return pl.pallas_call(
        _dt_kernel,
        grid=(t // tm,),
        in_specs=[pl.BlockSpec((nh, k), lambda i: (0, 0)),
                  pl.BlockSpec((tm, k), lambda i: (i, 0)),
                  pl.BlockSpec((nh, 1), lambda i: (0, 0))],
        out_specs=pl.BlockSpec((None, nh, tm), lambda i: (i // tpb, 0, i % tpb)),
        out_shape=jax.ShapeDtypeStruct((batch, nh, n_tok), F32),
        compiler_params=_params(("parallel",)),
        name="dt_proj",
    )(w_dt_t, h, dt_bias)


def _conv_kernel(prev_ref, cur_ref, next_ref, w_ref, b_ref, o_ref, win_ref, *, tr, first_tiles, last_tiles, tpb):
    t = pl.program_id(0) % tpb
    is_first = functools.reduce(jnp.logical_or, [t == v for v in first_tiles])
    is_last = functools.reduce(jnp.logical_or, [t == v for v in last_tiles])
    win_ref[0:CONV_HALO, :] = jnp.where(is_first, 0.0, prev_ref[...])
    win_ref[CONV_HALO:CONV_HALO + tr, :] = cur_ref[...]
    win_ref[CONV_HALO + tr:, :] = jnp.where(is_last, 0.0, next_ref[...])
    acc = b_ref[...]
    for k in range(SSD_CONV):
        start = CONV_HALO - SSD_CONV // 2 + k
        acc = acc + win_ref[start:start + tr, :] * w_ref[k:k + 1, :]
    o_ref[...] = _silu(acc).astype(o_ref.dtype)


def _conv_call(xbc, conv_w, conv_b, col0, ncols, out_dtype, n_tok, n_lat):
    t = xbc.shape[0]
    tr = _pick(math.gcd(n_lat, n_tok - n_lat), (256, 128))
    tc = 1024
    tpb = n_tok // tr
    first_tiles = (0, n_lat // tr)
    last_tiles = (n_lat // tr - 1, tpb - 1)
    cb0 = col0 // tc
    hb = tr // CONV_HALO
    nhb = t // CONV_HALO
    return pl.pallas_call(
        functools.partial(_conv_kernel, tr=tr, first_tiles=first_tiles, last_tiles=last_tiles, tpb=tpb),
        grid=(t // tr, ncols // tc),
        in_specs=[pl.BlockSpec((CONV_HALO, tc), lambda i, j: (jnp.maximum(i * hb - 1, 0), cb0 + j)),
                  pl.BlockSpec((tr, tc), lambda i, j: (i, cb0 + j)),
                  pl.BlockSpec((CONV_HALO, tc), lambda i, j: (jnp.minimum((i + 1) * hb, nhb - 1), cb0 + j)),
                  pl.BlockSpec((SSD_CONV, tc), lambda i, j: (0, cb0 + j)),
                  pl.BlockSpec((1, tc), lambda i, j: (0, cb0 + j))],
        out_specs=pl.BlockSpec((tr, tc), lambda i, j: (i, j)),
        out_shape=jax.ShapeDtypeStruct((t, ncols), out_dtype),
        scratch_shapes=[pltpu.VMEM((tr + 2 * CONV_HALO, tc), F32)],
        compiler_params=_params(("parallel", "arbitrary")),
        name="dwconv_silu",
    )(xbc, xbc, xbc, conv_w, conv_b)


def _attn_kernel(q_ref, k_ref, v_ref, lam_ref, g_ref, o_ref, *, tq, n_lat, n_tok, lam_init):
    lv = lam_ref[...]
    lam = (jnp.exp(jnp.sum(lv[0:1, :] * lv[1:2, :], axis=-1, keepdims=True))
           - jnp.exp(jnp.sum(lv[2:3, :] * lv[3:4, :], axis=-1, keepdims=True)) + lam_init)
    q = q_ref[...]
    lane = lax.broadcasted_iota(jnp.int32, q.shape, 1)
    zero = jnp.zeros_like(q)
    q1 = jnp.where(lane < ATT_HEAD_DIM, q, zero)
    q2 = jnp.where(lane < ATT_HEAD_DIM, zero, q)

    def softmax(s):
        e = jnp.exp(s - jnp.max(s, axis=-1, keepdims=True))
        return e * (1.0 / jnp.sum(e, axis=-1, keepdims=True))

    def attend(k0, k1):
        k = k_ref[k0:k1, :]
        nt = (((1,), (1,)), ((), ()))
        p1 = softmax(lax.dot_general(q1, k, nt, preferred_element_type=F32))
        p2 = softmax(lax.dot_general(q2, k, nt, preferred_element_type=F32))
        a = (p1 - lam * p2).astype(BF16)
        o = jnp.dot(a, v_ref[k0:k1, :], preferred_element_type=F32)
        o_ref[...] = (_rms(o, g_ref[...]) * (1.0 - lam_init)).astype(o_ref.dtype)

    t = pl.program_id(2)

    @pl.when(t < n_lat // tq)
    def _():
        attend(0, n_tok)

    @pl.when(t >= n_lat // tq)
    def _():
        attend(n_lat, n_tok)


def _attn_call(qk, vzg, v_col0, lam_vecs, subln_g, lam_init, n_tok, n_lat, batch):
    t = qk.shape[0]
    tq = _pick(math.gcd(n_lat, n_tok - n_lat), (256, 128))
    tpb = n_tok // tq
    vb0 = v_col0 // ATT_BLOCK
    return pl.pallas_call(
        functools.partial(_attn_kernel, tq=tq, n_lat=n_lat, n_tok=n_tok, lam_init=lam_init),
        grid=(batch, ATT_HEADS, tpb),
        in_specs=[pl.BlockSpec((tq, ATT_BLOCK), lambda b, h, i: (b * tpb + i, h)),
                  pl.BlockSpec((n_tok, ATT_BLOCK), lambda b, h, i: (b, ATT_HEADS + h)),
                  pl.BlockSpec((n_tok, ATT_BLOCK), lambda b, h, i: (b, vb0 + h)),
                  pl.BlockSpec((4, ATT_HEAD_DIM), lambda b, h, i: (0, 0)),
                  pl.BlockSpec((1, ATT_BLOCK), lambda b, h, i: (0, 0))],
        out_specs=pl.BlockSpec((tq, ATT_BLOCK), lambda b, h, i: (b * tpb + i, h)),
        out_shape=jax.ShapeDtypeStruct((t, ATT_HEADS * ATT_BLOCK), BF16),
        compiler_params=_params(("parallel", "parallel", "arbitrary"), VMEM_LIMIT),
        name="diff_attention",
    )(qk, qk, vzg, lam_vecs, subln_g)


def _split3(v):
    hi = v.astype(BF16)
    r1 = v - hi.astype(F32)
    mid = r1.astype(BF16)
    lo = (r1 - mid.astype(F32)).astype(BF16)
    return hi, mid, lo


def _ssd_kernel(xs_ref, b_ref, c_ref, dt_ref, alog_ref, d_ref, y_ref, st_ref, *, n_lat, n_tok):
    n_chunks = n_tok // SSD_CHUNK
    lat_chunks = n_lat // SSD_CHUNK
    gh = SSD_GROUP_HEADS
    a8 = -jnp.exp(alog_ref[...])
    row = lax.broadcasted_iota(jnp.int32, (SSD_CHUNK, SSD_CHUNK), 0)
    col = lax.broadcasted_iota(jnp.int32, (SSD_CHUNK, SSD_CHUNK), 1)
    lower = col <= row
    upper = col >= row
    head_of_lane = lax.broadcasted_iota(jnp.int32, (1, SSD_GROUP_COLS), 1) // SSD_HEAD_DIM
    pad_rows = jnp.zeros((SSD_CHUNK - 4 * gh, SSD_CHUNK), F32)

    def per_head(m, c0):
        out = m[:, c0 + gh - 1:c0 + gh]
        for h in range(gh - 2, -1, -1):
            out = jnp.where(head_of_lane == h, m[:, c0 + h:c0 + h + 1], out)
        return out

    def chunk(c, rev):
        r0 = pl.multiple_of(c * SSD_CHUNK, SSD_CHUNK)
        bc = b_ref[pl.ds(r0, SSD_CHUNK), :]
        cc = c_ref[pl.ds(r0, SSD_CHUNK), :]
        xc = xs_ref[pl.ds(r0, SSD_CHUNK), :]
        dt8 = dt_ref[:, pl.ds(r0, SSD_CHUNK)]
        scan_to = (upper if not rev else lower).astype(BF16)
        cs8 = sum(jnp.dot(p, scan_to, preferred_element_type=F32) for p in _split3(dt8 * a8))
        cols = jnp.concatenate([cs8, dt8, pad_rows], axis=0).T
        ho = gh if rev else 0
        end = 0 if rev else SSD_CHUNK - 1
        cs_x = per_head(cols, ho)
        dt_x = per_head(cols, 2 * gh + ho)
        tot_x = per_head(cols[end:end + 1, :], ho)
        xdt = xc * dt_x
        cb = lax.dot_general(cc, bc, (((1,), (1,)), ((), ())), preferred_element_type=F32)
        causal = upper if rev else lower
        ms, xm = [], []
        xdt_b = xdt.astype(BF16)
        for h in range(gh):
            seg = cols[:, ho + h:ho + h + 1] - cs8[ho + h:ho + h + 1, :]
            lmat = jnp.exp(jnp.where(causal, seg, -jnp.inf))
            ms.append((cb * lmat).astype(BF16))
            xm.append(jnp.where(head_of_lane == h, xdt_b, jnp.zeros_like(xdt_b)))
        y = jnp.dot(jnp.concatenate(ms, axis=1), jnp.concatenate(xm, axis=0), preferred_element_type=F32)
        st = st_ref[...]
        y = y + jnp.dot(cc, st.astype(BF16), preferred_element_type=F32) * jnp.exp(cs_x)
        xw = (xdt * jnp.exp(tot_x - cs_x)).astype(BF16)
        st_ref[...] = st * jnp.exp(tot_x) + lax.dot_general(
            bc, xw, (((0,), (0,)), ((), ())), preferred_element_type=F32)
        if rev:
            y_ref[pl.ds(r0, SSD_CHUNK), :] += y
        else:
            y_ref[pl.ds(r0, SSD_CHUNK), :] = y + d_ref[...] * xc

    st_ref[...] = jnp.zeros_like(st_ref)

    def fwd(i, carry):
        chunk(jnp.where(i < n_chunks - lat_chunks, lat_chunks + i, i - (n_chunks - lat_chunks)), False)
        return carry

    lax.fori_loop(0, n_chunks, fwd, 0)
    st_ref[...] = jnp.zeros_like(st_ref)

    def bwd(i, carry):
        chunk(n_chunks - 1 - i, True)
        return carry

    lax.fori_loop(0, n_chunks, bwd, 0)


def _ssd_call(xs, bcm, dt, alog, d_x, n_tok, n_lat, batch):
    t = xs.shape[0]
    gc = SSD_GROUP_COLS
    return pl.pallas_call(
        functools.partial(_ssd_kernel, n_lat=n_lat, n_tok=n_tok),
        grid=(batch, SSD_GROUPS),
        in_specs=[pl.BlockSpec((n_tok, gc), lambda b, g: (b, g)),
                  pl.BlockSpec((n_tok, SSD_STATE), lambda b, g: (b, g)),
                  pl.BlockSpec((n_tok, SSD_STATE), lambda b, g: (b, SSD_GROUPS + g)),
                  pl.BlockSpec((None, None, 2 * SSD_GROUP_HEADS, n_tok), lambda b, g: (b, g, 0, 0)),
                  pl.BlockSpec((None, 2 * SSD_GROUP_HEADS, SSD_CHUNK), lambda b, g: (g, 0, 0)),
                  pl.BlockSpec((1, gc), lambda b, g: (0, g))],
        out_specs=pl.BlockSpec((n_tok, gc), lambda b, g: (b, g)),
        out_shape=jax.ShapeDtypeStruct((t, SSD_D_INNER), F32),
        scratch_shapes=[pltpu.VMEM((SSD_STATE, gc), F32)],
        compiler_params=_params(("parallel", "arbitrary"), VMEM_LIMIT),
        name="ssd_bidir",
    )(xs, bcm, bcm, dt, alog, d_x)


def _mix_out_kernel(att_ref, y_ref, z_ref, ga_ref, gs_ref, x_ref, ml_ref, mc_ref, sg_ref, ng_ref,
                    wa_ref, ws_ref, wo_ref, xo_ref, h_ref, *, tm, tpb, n_lat):
    is_lat = _is_latent(tm, tpb, n_lat)
    ssd_n = _rms(y_ref[...] * _silu(z_ref[...].astype(F32)), sg_ref[...]).astype(BF16)
    acc_s = jnp.dot(ssd_n, ws_ref[...], preferred_element_type=F32)
    acc_a = jnp.dot(att_ref[...], wa_ref[...], preferred_element_type=F32)
    merged = (_sigmoid(ga_ref[...].astype(F32)) * acc_a + _sigmoid(gs_ref[...].astype(F32)) * acc_s).astype(BF16)
    o = jnp.dot(merged, wo_ref[...], preferred_element_type=F32)
    x_new = x_ref[...] + _mod(ml_ref, mc_ref, 2, is_lat) * o
    xo_ref[...] = x_new
    h = _rms(x_new, ng_ref[...]) * (1.0 + _mod(ml_ref, mc_ref, 4, is_lat)) + _mod(ml_ref, mc_ref, 3, is_lat)
    h_ref[...] = h.astype(h_ref.dtype)


def _mix_out_call(att, y, vzg, z_col0, ga_col0, gs_col0, x, mods, ssd_g, mlp_g, wa, ws, wo, n_tok, n_lat, batch):
    t, d = x.shape
    tm = _pick(n_tok, (384, 128))
    tpb = n_tok // tm
    row = lambda i: (i, 0)
    const = lambda i: (0, 0)
    return pl.pallas_call(
        functools.partial(_mix_out_kernel, tm=tm, tpb=tpb, n_lat=n_lat),
        grid=(t // tm,),
        in_specs=[pl.BlockSpec((tm, d), row),
                  pl.BlockSpec((tm, SSD_D_INNER), row),
                  pl.BlockSpec((tm, SSD_D_INNER), lambda i: (i, z_col0 // SSD_D_INNER)),
                  pl.BlockSpec((tm, d), lambda i: (i, ga_col0 // d)),
                  pl.BlockSpec((tm, d), lambda i: (i, gs_col0 // d)),
                  pl.BlockSpec((tm, d), row),
                  pl.BlockSpec((None, N_MOD, d), lambda i: (i // tpb, 0, 0)),
                  pl.BlockSpec((None, N_MOD, d), lambda i: (batch, 0, 0)),
                  pl.BlockSpec((1, SSD_D_INNER), const),
                  pl.BlockSpec((1, d), const),
                  pl.BlockSpec((d, d), const),
                  pl.BlockSpec((SSD_D_INNER, d), const),
                  pl.BlockSpec((d, d), const)],
        out_specs=[pl.BlockSpec((tm, d), row), pl.BlockSpec((tm, d), row)],
        out_shape=[jax.ShapeDtypeStruct((t, d), F32), jax.ShapeDtypeStruct((t, d), BF16)],
        compiler_params=_params(("parallel",), VMEM_LIMIT),
        name="mixer_out",
    )(att, y, vzg, vzg, vzg, x, mods, mods, ssd_g, mlp_g, wa, ws, wo)


def _mlp_kernel(h_ref, x_ref, ml_ref, mc_ref, w1_ref, w2_ref, o_ref, acc_ref, *, tm, tpb, n_lat):
    f = pl.program_id(1)

    @pl.when(f == 0)
    def _():
        acc_ref[...] = jnp.zeros_like(acc_ref)

    u = jnp.maximum(jnp.dot(h_ref[...], w1_ref[...], preferred_element_type=F32), 0.0)
    acc_ref[...] += jnp.dot((u * u).astype(BF16), w2_ref[...], preferred_element_type=F32)

    @pl.when(f == pl.num_programs(1) - 1)
    def _():
        is_lat = _is_latent(tm, tpb, n_lat)
        o_ref[...] = x_ref[...] + _mod(ml_ref, mc_ref, 5, is_lat) * acc_ref[...]


def _mlp_call(h, x, mods, w1, w2, n_tok, n_lat, batch):
    t, d = x.shape
    ff = w1.shape[1]
    tm = _pick(n_tok, (768, 384, 128))
    tpb = n_tok // tm
    tf = 512
    return pl.pallas_call(
        functools.partial(_mlp_kernel, tm=tm, tpb=tpb, n_lat=n_lat),
        grid=(t // tm, ff // tf),
        in_specs=[pl.BlockSpec((tm, d), lambda i, f: (i, 0)),
                  pl.BlockSpec((tm, d), lambda i, f: (i, 0)),
                  pl.BlockSpec((None, N_MOD, d), lambda i, f: (i // tpb, 0, 0)),
                  pl.BlockSpec((None, N_MOD, d), lambda i, f: (batch, 0, 0)),
                  pl.BlockSpec((d, tf), lambda i, f: (0, f)),
                  pl.BlockSpec((tf, d), lambda i, f: (f, 0))],
        out_specs=pl.BlockSpec((tm, d), lambda i, f: (i, 0)),
        out_shape=jax.ShapeDtypeStruct((t, d), F32),
        scratch_shapes=[pltpu.VMEM((tm, d), F32)],
        compiler_params=_params(("parallel", "arbitrary"), VMEM_LIMIT),
        name="sq_relu_mlp",
    )(h, x, mods, mods, w1, w2)


def _final_norm_kernel(x_ref, g_ref, o_ref):
    o_ref[...] = _rms(x_ref[...], g_ref[...])


def _final_norm_call(x, g, n_tok, n_lat, batch):
    d = x.shape[1]
    tr = _pick(math.gcd(n_lat, n_tok), (256, 128))
    lt = n_lat // tr
    tpb = n_tok // tr
    return pl.pallas_call(
        _final_norm_kernel,
        grid=(batch, lt),
        in_specs=[pl.BlockSpec((tr, d), lambda b, i: (b * tpb + i, 0)),
                  pl.BlockSpec((1, d), lambda b, i: (0, 0))],
        out_specs=pl.BlockSpec((None, tr, d), lambda b, i: (b, i, 0)),
        out_shape=jax.ShapeDtypeStruct((batch, n_lat, d), F32),
        compiler_params=_params(("parallel", "parallel")),
        name="final_norm",
    )(x, g)


def _rope_tables(n_lat, n_ctx):
    rows = n_lat // GRID_W
    row = jnp.broadcast_to(jnp.arange(rows)[:, None], (rows, GRID_W)).reshape(-1).astype(F32)
    col = jnp.broadcast_to(jnp.arange(GRID_W)[None, :], (rows, GRID_W)).reshape(-1).astype(F32)
    inv = jnp.float32(ROPE_BASE) ** (-jnp.arange(ROPE_PAIRS, dtype=F32) / ROPE_PAIRS)
    ang = jnp.concatenate([row[:, None] * inv, col[:, None] * inv], axis=-1)
    cos, sin = jnp.cos(ang), jnp.sin(ang)
    cos_t = jnp.concatenate([cos, cos, cos, cos], axis=-1)
    sin_t = jnp.concatenate([-sin, sin, -sin, sin], axis=-1)
    cos_t = jnp.concatenate([cos_t, jnp.ones((n_ctx, ATT_BLOCK), F32)], axis=0)
    sin_t = jnp.concatenate([sin_t, jnp.zeros((n_ctx, ATT_BLOCK), F32)], axis=0)
    return cos_t, sin_t


def kernel(x, c, ctx, c_ctx, ada_w, ada_b, norm_mix_g, w_in, conv_w, conv_b, dt_bias_f, dt_bias_b, a_log_f, a_log_b, ssd_d, ssd_norm_g, lambda_q1, lambda_k1, lambda_q2, lambda_k2, attn_subln_g, w_attn_o, w_ssd_o, w_out, norm_mlp_g, w_mlp1, w_mlp2, final_norm_g):
    batch, n_lat, d = x.shape
    n_ctx = ctx.shape[1]
    n_tok = n_lat + n_ctx
    depth = ada_w.shape[0]
    gh = SSD_GROUP_HEADS

    o_q, o_k, o_v = 0, 1024, 2048
    o_z = 3072
    o_xbc = o_z + SSD_D_INNER
    o_dt = o_xbc + SSD_D_INNER + SSD_BC
    o_ga = o_dt + 2 * SSD_HEADS
    o_gs = o_ga + D_MODEL

    scale = ATT_HEAD_DIM ** -0.5
    w_qk = jnp.concatenate([w_in[:, :, o_q:o_k] * scale, w_in[:, :, o_k:o_v]], axis=-1).astype(BF16)
    w_zvg = jnp.concatenate([w_in[:, :, o_z:o_xbc], w_in[:, :, o_v:o_z], w_in[:, :, o_ga:]], axis=-1).astype(BF16)
    z_col0, v_col0, ga_col0, gs_col0 = 0, SSD_D_INNER, SSD_D_INNER + D_MODEL, SSD_D_INNER + 2 * D_MODEL
    w_xbc = w_in[:, :, o_xbc:o_dt].astype(BF16)
    perm = jnp.asarray([f * SSD_HEADS + g * gh + h for g in range(SSD_GROUPS) for f in range(2) for h in range(gh)])
    w_dt_t = jnp.swapaxes(w_in[:, :, o_dt:o_ga][:, :, perm], 1, 2).astype(BF16)
    dt_bias = jnp.concatenate([dt_bias_f, dt_bias_b], axis=-1)[:, perm][:, :, None]
    a_log = jnp.concatenate([a_log_f, a_log_b], axis=-1)[:, perm].reshape(depth, SSD_GROUPS, 2 * gh, 1)
    a_log = jnp.broadcast_to(a_log, (depth, SSD_GROUPS, 2 * gh, SSD_CHUNK)).astype(F32)
    d_x = jnp.repeat(ssd_d, SSD_HEAD_DIM, axis=-1)[:, None, :].astype(F32)
    wa_b, ws_b, wo_b = w_attn_o.astype(BF16), w_ssd_o.astype(BF16), w_out.astype(BF16)
    w1_b, w2_b = w_mlp1.astype(BF16), w_mlp2.astype(BF16)
    lam_vecs = jnp.stack([lambda_q1, lambda_k1, lambda_q2, lambda_k2], axis=1).astype(F32)
    rope = _rope_tables(n_lat, n_ctx)

    rows_pad = -(-(batch + 1) // MOD_ROWS_PAD) * MOD_ROWS_PAD
    c_all = jnp.concatenate([c, c_ctx[None, :], jnp.zeros((rows_pad - batch - 1, d), F32)], axis=0)
    mods = _mod_call(c_all, ada_w, ada_b).reshape(depth, rows_pad, N_MOD, d)

    xt = jnp.concatenate([x, ctx], axis=1).reshape(batch * n_tok, d)
    for l in range(depth):
        lam_init = 0.8 - 0.6 * math.exp(-0.3 * l)
        h = _norm_mod_call(xt, mods[l], norm_mix_g[l][None, :], n_tok, n_lat, batch)
        qk = _proj_call(h, w_qk[l], BF16, n_tok, rope=rope)
        zvg = _proj_call(h, w_zvg[l], BF16, n_tok)
        xbc = _proj_call(h, w_xbc[l], F32, n_tok)
        dt = _dt_call(h, w_dt_t[l], dt_bias[l], n_tok, batch)
        dt = dt.reshape(batch, SSD_GROUPS, 2 * gh, n_tok)
        xs = _conv_call(xbc, conv_w[l], conv_b[l][None, :], 0, SSD_D_INNER, F32, n_tok, n_lat)
        bcm = _conv_call(xbc, conv_w[l], conv_b[l][None, :], SSD_D_INNER, SSD_BC, BF16, n_tok, n_lat)
        att = _attn_call(qk, zvg, v_col0, lam_vecs[l], attn_subln_g[l][None, :], lam_init, n_tok, n_lat, batch)
        y = _ssd_call(xs, bcm, dt, a_log[l], d_x[l], n_tok, n_lat, batch)
        xt, h2 = _mix_out_call(att, y, zvg, z_col0, ga_col0, gs_col0, xt, mods[l], ssd_norm_g[l][None, :],
                               norm_mlp_g[l][None, :], wa_b[l], ws_b[l], wo_b[l], n_tok, n_lat, batch)
        xt = _mlp_call(h2, xt, mods[l], w1_b[l], w2_b[l], n_tok, n_lat, batch)
    return _final_norm_call(xt, final_norm_g[None, :], n_tok, n_lat, batch)
```

```python
import functools
import math

import jax
import jax.numpy as jnp
from jax import lax
from jax.experimental import pallas as pl
from jax.experimental.pallas import tpu as pltpu

F32 = jnp.float32
BF16 = jnp.bfloat16

D_MODEL = 1024
GRID_W = 64
ATT_HEADS = 8
ATT_HEAD_DIM = 64
ATT_BLOCK = 2 * ATT_HEAD_DIM
ROPE_BASE = 10000.0
ROPE_PAIRS = ATT_HEAD_DIM // 4
SSD_D_INNER = 2 * D_MODEL
SSD_HEAD_DIM = 64
SSD_HEADS = SSD_D_INNER // SSD_HEAD_DIM
SSD_GROUPS = 8
SSD_GROUP_HEADS = SSD_HEADS // SSD_GROUPS
SSD_GROUP_COLS = SSD_GROUP_HEADS * SSD_HEAD_DIM
SSD_STATE = 128
SSD_CONV = 5
SSD_CHUNK = 128
SSD_BC = 2 * SSD_GROUPS * SSD_STATE
D_FF = 4 * D_MODEL
N_MOD = 6
EPS = 1e-6
CONV_HALO = 8
MOD_ROWS_PAD = 8
VMEM_LIMIT = 56 * 1024 * 1024


def _params(sem, vmem=None):
    return pltpu.CompilerParams(dimension_semantics=sem, vmem_limit_bytes=vmem)


def _pick(n, candidates):
    for c in candidates:
        if n % c == 0:
            return c
    raise ValueError(f"no tile in {candidates} divides {n}")


def _silu(v):
    return v * (1.0 / (1.0 + jnp.exp(-v)))


def _sigmoid(v):
    return 1.0 / (1.0 + jnp.exp(-v))


def _rms(v, g):
    return v * lax.rsqrt(jnp.mean(v * v, axis=-1, keepdims=True) + EPS) * g


def _is_latent(tm, tiles_per_batch, n_lat):
    off = (pl.program_id(0) % tiles_per_batch) * tm
    rows = lax.broadcasted_iota(jnp.int32, (tm, 1), 0)
    return rows < (n_lat - off)


def _mod(ml_ref, mc_ref, k, is_lat):
    return jnp.where(is_lat, ml_ref[k:k + 1, :], mc_ref[k:k + 1, :])


def _mod_kernel(c_ref, w_ref, b_ref, o_ref):
    a = _silu(c_ref[...]).astype(BF16)
    o_ref[...] = jnp.dot(a, w_ref[...].astype(BF16), preferred_element_type=F32) + b_ref[...]


def _mod_call(c_all, ada_w, ada_b):
    depth, d, nm = ada_w.shape
    rows = c_all.shape[0]
    tn = 1024
    return pl.pallas_call(
        _mod_kernel,
        grid=(depth, nm // tn),
        in_specs=[pl.BlockSpec((rows, d), lambda l, j: (0, 0)),
                  pl.BlockSpec((None, d, tn), lambda l, j: (l, 0, j)),
                  pl.BlockSpec((None, 1, tn), lambda l, j: (l, 0, j))],
        out_specs=pl.BlockSpec((None, rows, tn), lambda l, j: (l, 0, j)),
        out_shape=jax.ShapeDtypeStruct((depth, rows, nm), F32),
        compiler_params=_params(("arbitrary", "arbitrary")),
        name="adaln_mod",
    )(c_all, ada_w, ada_b.reshape(depth, 1, nm))


def _norm_mod_kernel(x_ref, ml_ref, mc_ref, g_ref, o_ref, *, tm, tpb, n_lat):
    is_lat = _is_latent(tm, tpb, n_lat)
    h = _rms(x_ref[...], g_ref[...]) * (1.0 + _mod(ml_ref, mc_ref, 1, is_lat)) + _mod(ml_ref, mc_ref, 0, is_lat)
    o_ref[...] = h.astype(o_ref.dtype)


def _norm_mod_call(x, mods, g, n_tok, n_lat, batch):
    t, d = x.shape
    tm = _pick(n_tok, (768, 384, 128))
    tpb = n_tok // tm
    return pl.pallas_call(
        functools.partial(_norm_mod_kernel, tm=tm, tpb=tpb, n_lat=n_lat),
        grid=(t // tm,),
        in_specs=[pl.BlockSpec((tm, d), lambda i: (i, 0)),
                  pl.BlockSpec((None, N_MOD, d), lambda i: (i // tpb, 0, 0)),
                  pl.BlockSpec((None, N_MOD, d), lambda i: (batch, 0, 0)),
                  pl.BlockSpec((1, d), lambda i: (0, 0))],
        out_specs=pl.BlockSpec((tm, d), lambda i: (i, 0)),
        out_shape=jax.ShapeDtypeStruct((t, d), BF16),
        compiler_params=_params(("parallel",)),
        name="norm_modulate",
    )(x, mods, mods, g)


def _proj_kernel(a_ref, w_ref, o_ref):
    o_ref[...] = jnp.dot(a_ref[...], w_ref[...], preferred_element_type=F32).astype(o_ref.dtype)


def _proj_rope_kernel(a_ref, w_ref, cos_ref, sin_ref, o_ref, *, tn):
    acc = jnp.dot(a_ref[...], w_ref[...], preferred_element_type=F32)
    cos = cos_ref[...]
    sin = sin_ref[...]
    lane = lax.broadcasted_iota(jnp.int32, cos.shape, 1)
    first_half = (lane % ATT_HEAD_DIM) < (ATT_HEAD_DIM // 2)
    for j in range(tn // ATT_BLOCK):
        blk = acc[:, j * ATT_BLOCK:(j + 1) * ATT_BLOCK]
        partner = jnp.where(first_half,
                            pltpu.roll(blk, ATT_BLOCK - ATT_HEAD_DIM // 2, 1),
                            pltpu.roll(blk, ATT_HEAD_DIM // 2, 1))
        o_ref[:, j * ATT_BLOCK:(j + 1) * ATT_BLOCK] = (blk * cos + partner * sin).astype(o_ref.dtype)


def _proj_call(h, w, out_dtype, n_tok, rope=None):
    t, k = h.shape
    n = w.shape[1]
    tm = _pick(n_tok, (1152, 384, 128))
    tpb = n_tok // tm
    tn = _pick(n, (1024, 512, 256, 128))
    in_specs = [pl.BlockSpec((tm, k), lambda i, j: (i, 0)),
                pl.BlockSpec((k, tn), lambda i, j: (0, j))]
    args = [h, w]
    if rope is None:
        body = _proj_kernel
    else:
        body = functools.partial(_proj_rope_kernel, tn=tn)
        in_specs += [pl.BlockSpec((tm, ATT_BLOCK), lambda i, j: (i % tpb, 0)),
                     pl.BlockSpec((tm, ATT_BLOCK), lambda i, j: (i % tpb, 0))]
        args += list(rope)
    return pl.pallas_call(
        body,
        grid=(t // tm, n // tn),
        in_specs=in_specs,
        out_specs=pl.BlockSpec((tm, tn), lambda i, j: (i, j)),
        out_shape=jax.ShapeDtypeStruct((t, n), out_dtype),
        compiler_params=_params(("parallel", "arbitrary"), VMEM_LIMIT),
        name="in_proj_rope" if rope is not None else "in_proj",
    )(*args)


def _dt_kernel(w_ref, h_ref, b_ref, o_ref):
    raw = lax.dot_general(w_ref[...], h_ref[...], (((1,), (1,)), ((), ())), preferred_element_type=F32)
    v = raw + b_ref[...]
    o_ref[...] = jnp.maximum(v, 0.0) + jnp.log1p(jnp.exp(-jnp.abs(v)))


def _dt_call(h, w_dt_t, dt_bias, n_tok, batch):
    t, k = h.shape
    nh = w_dt_t.shape[0]
    tm = _pick(n_tok, (1152, 384, 128))
    tpb = n_tok // tm
    return pl.pallas_call(
        _dt_kernel,
        grid=(t // tm,),
        in_specs=[pl.BlockSpec((nh, k), lambda i: (0, 0)),
                  pl.BlockSpec((tm, k), lambda i: (i, 0)),
                  pl.BlockSpec((nh, 1), lambda i: (0, 0))],
        out_specs=pl.BlockSpec((None, nh, tm), lambda i: (i // tpb, 0, i % tpb)),
        out_shape=jax.ShapeDtypeStruct((batch, nh, n_tok), F32),
        compiler_params=_params(("parallel",)),
        name="dt_proj",
    )(w_dt_t, h, dt_bias)


def _conv_kernel(prev_ref, cur_ref, next_ref, w_ref, b_ref, o_ref, win_ref, *, tr, first_tiles, last_tiles, tpb):
    t = pl.program_id(0) % tpb
    is_first = functools.reduce(jnp.logical_or, [t == v for v in first_tiles])
    is_last = functools.reduce(jnp.logical_or, [t == v for v in last_tiles])
    win_ref[0:CONV_HALO, :] = jnp.where(is_first, 0.0, prev_ref[...])
    win_ref[CONV_HALO:CONV_HALO + tr, :] = cur_ref[...]
    win_ref[CONV_HALO + tr:, :] = jnp.where(is_last, 0.0, next_ref[...])
    acc = b_ref[...]
    for k in range(SSD_CONV):
        start = CONV_HALO - SSD_CONV // 2 + k
        acc = acc + win_ref[start:start + tr, :] * w_ref[k:k + 1, :]
    o_ref[...] = _silu(acc).astype(o_ref.dtype)


def _conv_call(xbc, conv_w, conv_b, col0, ncols, out_dtype, n_tok, n_lat):
    t = xbc.shape[0]
    tr = _pick(math.gcd(n_lat, n_tok - n_lat), (256, 128))
    tc = 1024
    tpb = n_tok // tr
    first_tiles = (0, n_lat // tr)
    last_tiles = (n_lat // tr - 1, tpb - 1)
    cb0 = col0 // tc
    hb = tr // CONV_HALO
    nhb = t // CONV_HALO
    return pl.pallas_call(
        functools.partial(_conv_kernel, tr=tr, first_tiles=first_tiles, last_tiles=last_tiles, tpb=tpb),
        grid=(t // tr, ncols // tc),
        in_specs=[pl.BlockSpec((CONV_HALO, tc), lambda i, j: (jnp.maximum(i * hb - 1, 0), cb0 + j)),
                  pl.BlockSpec((tr, tc), lambda i, j: (i, cb0 + j)),
                  pl.BlockSpec((CONV_HALO, tc), lambda i, j: (jnp.minimum((i + 1) * hb, nhb - 1), cb0 + j)),
                  pl.BlockSpec((SSD_CONV, tc), lambda i, j: (0, cb0 + j)),
                  pl.BlockSpec((1, tc), lambda i, j: (0, cb0 + j))],
        out_specs=pl.BlockSpec((tr, tc), lambda i, j: (i, j)),
        out_shape=jax.ShapeDtypeStruct((t, ncols), out_dtype),
        scratch_shapes=[pltpu.VMEM((tr + 2 * CONV_HALO, tc), F32)],
        compiler_params=_params(("parallel", "arbitrary")),
        name="dwconv_silu",
    )(xbc, xbc, xbc, conv_w, conv_b)


def _attn_kernel(q_ref, k_ref, v_ref, lam_ref, g_ref, o_ref, *, tq, n_lat, n_tok, lam_init):
    lv = lam_ref[...]
    lam = (jnp.exp(jnp.sum(lv[0:1, :] * lv[1:2, :], axis=-1, keepdims=True))
           - jnp.exp(jnp.sum(lv[2:3, :] * lv[3:4, :], axis=-1, keepdims=True)) + lam_init)
    q = q_ref[...]
    lane = lax.broadcasted_iota(jnp.int32, q.shape, 1)
    zero = jnp.zeros_like(q)
    q1 = jnp.where(lane < ATT_HEAD_DIM, q, zero)
    q2 = jnp.where(lane < ATT_HEAD_DIM, zero, q)

    def exp_rows(s):
        e = jnp.exp2(s - jnp.max(s, axis=-1, keepdims=True))
        return e, jnp.sum(e, axis=-1, keepdims=True)

    def attend(k0, k1):
        k = k_ref[k0:k1, :]
        nt = (((1,), (1,)), ((), ()))
        e1, l1 = exp_rows(lax.dot_general(q1, k, nt, preferred_element_type=F32))
        e2, l2 = exp_rows(lax.dot_general(q2, k, nt, preferred_element_type=F32))
        a = (e1 - (lam * l1 / l2) * e2).astype(BF16)
        o = jnp.dot(a, v_ref[k0:k1, :], preferred_element_type=F32) * (1.0 / l1)
        o_ref[...] = (_rms(o, g_ref[...]) * (1.0 - lam_init)).astype(o_ref.dtype)

    t = pl.program_id(2)

    @pl.when(t < n_lat // tq)
    def _():
        attend(0, n_tok)

    @pl.when(t >= n_lat // tq)
    def _():
        attend(n_lat, n_tok)


def _attn_call(qk, vzg, v_col0, lam_vecs, subln_g, lam_init, n_tok, n_lat, batch):
    t = qk.shape[0]
    tq = _pick(math.gcd(n_lat, n_tok - n_lat), (256, 128))
    tpb = n_tok // tq
    vb0 = v_col0 // ATT_BLOCK
    return pl.pallas_call(
        functools.partial(_attn_kernel, tq=tq, n_lat=n_lat, n_tok=n_tok, lam_init=lam_init),
        grid=(batch, ATT_HEADS, tpb),
        in_specs=[pl.BlockSpec((tq, ATT_BLOCK), lambda b, h, i: (b * tpb + i, h)),
                  pl.BlockSpec((n_tok, ATT_BLOCK), lambda b, h, i: (b, ATT_HEADS + h)),
                  pl.BlockSpec((n_tok, ATT_BLOCK), lambda b, h, i: (b, vb0 + h)),
                  pl.BlockSpec((4, ATT_HEAD_DIM), lambda b, h, i: (0, 0)),
                  pl.BlockSpec((1, ATT_BLOCK), lambda b, h, i: (0, 0))],
        out_specs=pl.BlockSpec((tq, ATT_BLOCK), lambda b, h, i: (b * tpb + i, h)),
        out_shape=jax.ShapeDtypeStruct((t, ATT_HEADS * ATT_BLOCK), BF16),
        compiler_params=_params(("parallel", "parallel", "arbitrary"), VMEM_LIMIT),
        name="diff_attention",
    )(qk, qk, vzg, lam_vecs, subln_g)


def _split3(v):
    hi = v.astype(BF16)
    r1 = v - hi.astype(F32)
    mid = r1.astype(BF16)
    lo = (r1 - mid.astype(F32)).astype(BF16)
    return hi, mid, lo


def _ssd_constants():
    gh, gc, n = SSD_GROUP_HEADS, SSD_GROUP_COLS, SSD_CHUNK
    j = jnp.arange(n)
    tri2 = jnp.concatenate([j[:, None] <= j[None, :], j[:, None] >= j[None, :]], axis=1)
    lane = jnp.arange(4 * gc)
    expand = j[:, None] == (2 * gh + lane // SSD_HEAD_DIM)[None, :]
    return tri2.astype(BF16), expand.astype(BF16)


def _ssd_kernel(xs_ref, b_ref, c_ref, dt_ref, alog_ref, d_ref, tri2_ref, expand_ref,
                y_ref, st_ref, sst_ref, ecs_ref, dec_ref, hin_ref, *, n_lat, n_tok):
    n_chunks = n_tok // SSD_CHUNK
    lat_chunks = n_lat // SSD_CHUNK
    ctx_chunks = n_chunks - lat_chunks
    gh, gc, n = SSD_GROUP_HEADS, SSD_GROUP_COLS, SSD_CHUNK
    a8 = -jnp.exp(alog_ref[...]) * math.log2(math.e)
    row = lax.broadcasted_iota(jnp.int32, (n, n), 0)
    col = lax.broadcasted_iota(jnp.int32, (n, n), 1)
    below, above = col < row, col > row
    fwd_rows = lax.broadcasted_iota(jnp.int32, (2 * gh, n), 0) < gh
    head_of_lane = lax.broadcasted_iota(jnp.int32, (1, gc), 1) // SSD_HEAD_DIM
    pad_rows = jnp.zeros((n - 6 * gh, n), F32)

    def mm3(v, m):
        return sum(jnp.dot(p, m, preferred_element_type=F32) for p in _split3(v))

    def rows(c):
        return pl.ds(pl.multiple_of(c * n, n), n)

    group = next(k for k in (3, 2, 1) if n_chunks % k == 0)

    def local(i, carry):
        cs = [i * group + k for k in range(group)]
        dt8s = [dt_ref[:, rows(c)] for c in cs]
        cs2s = [mm3(dt8 * a8, tri2_ref[...]) for dt8 in dt8s]
        cs8s, colss, pieces = [], [], []
        for dt8, cs2 in zip(dt8s, cs2s):
            cs8 = jnp.where(fwd_rows, cs2[:, :n], cs2[:, n:])
            tot = jnp.where(fwd_rows, cs8[:, n - 1:n], cs8[:, 0:1])
            e8 = jnp.exp2(cs8)
            w8 = dt8 * jnp.exp2(tot - cs8)
            cols = jnp.concatenate([cs8, e8, w8, pad_rows], axis=0).T
            hi = cols.astype(BF16)
            cs8s.append(cs8)
            colss.append(cols)
            pieces.append((hi, (cols - hi.astype(F32)).astype(BF16)))
        ews = [jnp.dot(hi, expand_ref[...], preferred_element_type=F32)
               + jnp.dot(mid, expand_ref[...], preferred_element_type=F32)
               for hi, mid in pieces]
        bcs = [b_ref[rows(c), :] for c in cs]
        xcs = [xs_ref[rows(c), :] for c in cs]
        for c, ew, bc, xc in zip(cs, ews, bcs, xcs):
            ecs_ref[rows(c), :] = ew[:, :2 * gc]
            dec_ref[c] = jnp.broadcast_to(
                jnp.concatenate([ew[n - 1:n, :gc], ew[0:1, gc:2 * gc]], axis=1), (8, 2 * gc))
            xw = (jnp.concatenate([xc, xc], axis=1) * ew[:, 2 * gc:]).astype(BF16)
            sst_ref[c] = lax.dot_general(bc, xw, (((0,), (0,)), ((), ())), preferred_element_type=F32)
        cbs = [lax.dot_general(c_ref[rows(c), :], bc, (((1,), (1,)), ((), ())), preferred_element_type=F32)
               for c, bc in zip(cs, bcs)]
        for c, cb, cs8, cols, dt8, xc in zip(cs, cbs, cs8s, colss, dt8s, xcs):
            xb = xc.astype(BF16)
            dsum = dt8[:gh, :] + dt8[gh:, :]
            ms, xm = [], []
            for h in range(gh):
                seg = jnp.where(below, cols[:, h:h + 1] - cs8[h:h + 1, :],
                                cols[:, gh + h:gh + h + 1] - cs8[gh + h:gh + h + 1, :])
                dts = jnp.where(below, dt8[h:h + 1, :], jnp.where(above, dt8[gh + h:gh + h + 1, :], dsum[h:h + 1, :]))
                ms.append((cb * jnp.exp2(seg) * dts).astype(BF16))
                xm.append(jnp.where(head_of_lane == h, xb, jnp.zeros_like(xb)))
            y = jnp.dot(jnp.concatenate(ms, axis=1), jnp.concatenate(xm, axis=0), preferred_element_type=F32)
            y_ref[rows(c), :] = y + d_ref[...] * xc
        return carry

    lax.fori_loop(0, n_chunks // group, local, 0)

    st_ref[...] = jnp.zeros_like(st_ref)

    def scan(i, carry):
        cf = jnp.where(i < ctx_chunks, lat_chunks + i, i - ctx_chunks)
        cr = n_chunks - 1 - i
        st = st_ref[...]
        hin_ref[cf, :, :gc] = st[:, :gc].astype(BF16)
        hin_ref[cr, :, gc:] = st[:, gc:].astype(BF16)
        st_ref[:, :gc] = st[:, :gc] * dec_ref[cf][0:1, :gc] + sst_ref[cf][:, :gc]
        st_ref[:, gc:] = st[:, gc:] * dec_ref[cr][0:1, gc:] + sst_ref[cr][:, gc:]
        return carry

    lax.fori_loop(0, n_chunks, scan, 0)

    def inter(i, carry):
        cs = [i * group + k for k in range(group)]
        ts = [jnp.dot(c_ref[rows(c), :], hin_ref[c], preferred_element_type=F32) for c in cs]
        for c, t in zip(cs, ts):
            t = t * ecs_ref[rows(c), :]
            y_ref[rows(c), :] += t[:, :gc] + t[:, gc:]
        return carry

    lax.fori_loop(0, n_chunks // group, inter, 0)


def _ssd_call(xs, bcm, dt, alog, d_x, n_tok, n_lat, batch):
    t = xs.shape[0]
    gc = SSD_GROUP_COLS
    n_chunks = n_tok // SSD_CHUNK
    consts = _ssd_constants()
    const_specs = [pl.BlockSpec(m.shape, lambda b, g: (0, 0)) for m in consts]
    return pl.pallas_call(
        functools.partial(_ssd_kernel, n_lat=n_lat, n_tok=n_tok),
        grid=(batch, SSD_GROUPS),
        in_specs=[pl.BlockSpec((n_tok, gc), lambda b, g: (b, g)),
                  pl.BlockSpec((n_tok, SSD_STATE), lambda b, g: (b, g)),
                  pl.BlockSpec((n_tok, SSD_STATE), lambda b, g: (b, SSD_GROUPS + g)),
                  pl.BlockSpec((None, None, 2 * SSD_GROUP_HEADS, n_tok), lambda b, g: (b, g, 0, 0)),
                  pl.BlockSpec((None, 2 * SSD_GROUP_HEADS, SSD_CHUNK), lambda b, g: (g, 0, 0)),
                  pl.BlockSpec((1, gc), lambda b, g: (0, g))] + const_specs,
        out_specs=pl.BlockSpec((n_tok, gc), lambda b, g: (b, g)),
        out_shape=jax.ShapeDtypeStruct((t, SSD_D_INNER), F32),
        scratch_shapes=[pltpu.VMEM((SSD_STATE, 2 * gc), F32),
                        pltpu.VMEM((n_chunks, SSD_STATE, 2 * gc), F32),
                        pltpu.VMEM((n_tok, 2 * gc), F32),
                        pltpu.VMEM((n_chunks, 8, 2 * gc), F32),
                        pltpu.VMEM((n_chunks, SSD_STATE, 2 * gc), BF16)],
        compiler_params=_params(("parallel", "arbitrary"), VMEM_LIMIT),
        name="ssd_bidir",
    )(xs, bcm, bcm, dt, alog, d_x, *consts)


def _mix_out_kernel(att_ref, y_ref, z_ref, ga_ref, gs_ref, x_ref, ml_ref, mc_ref, sg_ref, ng_ref,
                    wa_ref, ws_ref, wo_ref, xo_ref, h_ref, *, tm, tpb, n_lat):
    is_lat = _is_latent(tm, tpb, n_lat)
    ssd_n = _rms(y_ref[...] * _silu(z_ref[...].astype(F32)), sg_ref[...]).astype(BF16)
    acc_s = jnp.dot(ssd_n, ws_ref[...], preferred_element_type=F32)
    acc_a = jnp.dot(att_ref[...], wa_ref[...], preferred_element_type=F32)
    merged = (_sigmoid(ga_ref[...].astype(F32)) * acc_a + _sigmoid(gs_ref[...].astype(F32)) * acc_s).astype(BF16)
    o = jnp.dot(merged, wo_ref[...], preferred_element_type=F32)
    x_new = x_ref[...] + _mod(ml_ref, mc_ref, 2, is_lat) * o
    xo_ref[...] = x_new
    h = _rms(x_new, ng_ref[...]) * (1.0 + _mod(ml_ref, mc_ref, 4, is_lat)) + _mod(ml_ref, mc_ref, 3, is_lat)
    h_ref[...] = h.astype(h_ref.dtype)


def _mix_out_call(att, y, vzg, z_col0, ga_col0, gs_col0, x, mods, ssd_g, mlp_g, wa, ws, wo, n_tok, n_lat, batch):
    t, d = x.shape
    tm = _pick(n_tok, (384, 128))
    tpb = n_tok // tm
    row = lambda i: (i, 0)
    const = lambda i: (0, 0)
    return pl.pallas_call(
        functools.partial(_mix_out_kernel, tm=tm, tpb=tpb, n_lat=n_lat),
        grid=(t // tm,),
        in_specs=[pl.BlockSpec((tm, d), row),
                  pl.BlockSpec((tm, SSD_D_INNER), row),
                  pl.BlockSpec((tm, SSD_D_INNER), lambda i: (i, z_col0 // SSD_D_INNER)),
                  pl.BlockSpec((tm, d), lambda i: (i, ga_col0 // d)),
                  pl.BlockSpec((tm, d), lambda i: (i, gs_col0 // d)),
                  pl.BlockSpec((tm, d), row),
                  pl.BlockSpec((None, N_MOD, d), lambda i: (i // tpb, 0, 0)),
                  pl.BlockSpec((None, N_MOD, d), lambda i: (batch, 0, 0)),
                  pl.BlockSpec((1, SSD_D_INNER), const),
                  pl.BlockSpec((1, d), const),
                  pl.BlockSpec((d, d), const),
                  pl.BlockSpec((SSD_D_INNER, d), const),
                  pl.BlockSpec((d, d), const)],
        out_specs=[pl.BlockSpec((tm, d), row), pl.BlockSpec((tm, d), row)],
        out_shape=[jax.ShapeDtypeStruct((t, d), F32), jax.ShapeDtypeStruct((t, d), BF16)],
        compiler_params=_params(("parallel",), VMEM_LIMIT),
        name="mixer_out",
    )(att, y, vzg, vzg, vzg, x, mods, mods, ssd_g, mlp_g, wa, ws, wo)


def _mlp_kernel(h_ref, x_ref, ml_ref, mc_ref, w1_ref, w2_ref, o_ref, acc_ref, *, tm, tpb, n_lat):
    f = pl.program_id(1)

    @pl.when(f == 0)
    def _():
        acc_ref[...] = jnp.zeros_like(acc_ref)

    u = jnp.maximum(jnp.dot(h_ref[...], w1_ref[...], preferred_element_type=F32), 0.0)
    acc_ref[...] += jnp.dot((u * u).astype(BF16), w2_ref[...], preferred_element_type=F32)

    @pl.when(f == pl.num_programs(1) - 1)
    def _():
        is_lat = _is_latent(tm, tpb, n_lat)
        o_ref[...] = x_ref[...] + _mod(ml_ref, mc_ref, 5, is_lat) * acc_ref[...]


def _mlp_call(h, x, mods, w1, w2, n_tok, n_lat, batch):
    t, d = x.shape
    ff = w1.shape[1]
    tm = _pick(n_tok, (768, 384, 128))
    tpb = n_tok // tm
    tf = 512
    return pl.pallas_call(
        functools.partial(_mlp_kernel, tm=tm, tpb=tpb, n_lat=n_lat),
        grid=(t // tm, ff // tf),
        in_specs=[pl.BlockSpec((tm, d), lambda i, f: (i, 0)),
                  pl.BlockSpec((tm, d), lambda i, f: (i, 0)),
                  pl.BlockSpec((None, N_MOD, d), lambda i, f: (i // tpb, 0, 0)),
                  pl.BlockSpec((None, N_MOD, d), lambda i, f: (batch, 0, 0)),
                  pl.BlockSpec((d, tf), lambda i, f: (0, f)),
                  pl.BlockSpec((tf, d), lambda i, f: (f, 0))],
        out_specs=pl.BlockSpec((tm, d), lambda i, f: (i, 0)),
        out_shape=jax.ShapeDtypeStruct((t, d), F32),
        scratch_shapes=[pltpu.VMEM((tm, d), F32)],
        compiler_params=_params(("parallel", "arbitrary"), VMEM_LIMIT),
        name="sq_relu_mlp",
    )(h, x, mods, mods, w1, w2)


def _final_norm_kernel(x_ref, g_ref, o_ref):
    o_ref[...] = _rms(x_ref[...], g_ref[...])


def _final_norm_call(x, g, n_tok, n_lat, batch):
    d = x.shape[1]
    tr = _pick(math.gcd(n_lat, n_tok), (256, 128))
    lt = n_lat // tr
    tpb = n_tok // tr
    return pl.pallas_call(
        _final_norm_kernel,
        grid=(batch, lt),
        in_specs=[pl.BlockSpec((tr, d), lambda b, i: (b * tpb + i, 0)),
                  pl.BlockSpec((1, d), lambda b, i: (0, 0))],
        out_specs=pl.BlockSpec((None, tr, d), lambda b, i: (b, i, 0)),
        out_shape=jax.ShapeDtypeStruct((batch, n_lat, d), F32),
        compiler_params=_params(("parallel", "parallel")),
        name="final_norm",
    )(x, g)


def _rope_tables(n_lat, n_ctx):
    rows = n_lat // GRID_W
    row = jnp.broadcast_to(jnp.arange(rows)[:, None], (rows, GRID_W)).reshape(-1).astype(F32)
    col = jnp.broadcast_to(jnp.arange(GRID_W)[None, :], (rows, GRID_W)).reshape(-1).astype(F32)
    inv = jnp.float32(ROPE_BASE) ** (-jnp.arange(ROPE_PAIRS, dtype=F32) / ROPE_PAIRS)
    ang = jnp.concatenate([row[:, None] * inv, col[:, None] * inv], axis=-1)
    cos, sin = jnp.cos(ang), jnp.sin(ang)
    cos_t = jnp.concatenate([cos, cos, cos, cos], axis=-1)
    sin_t = jnp.concatenate([-sin, sin, -sin, sin], axis=-1)
    cos_t = jnp.concatenate([cos_t, jnp.ones((n_ctx, ATT_BLOCK), F32)], axis=0)
    sin_t = jnp.concatenate([sin_t, jnp.zeros((n_ctx, ATT_BLOCK), F32)], axis=0)
    return cos_t, sin_t


def kernel(x, c, ctx, c_ctx, ada_w, ada_b, norm_mix_g, w_in, conv_w, conv_b, dt_bias_f, dt_bias_b, a_log_f, a_log_b, ssd_d, ssd_norm_g, lambda_q1, lambda_k1, lambda_q2, lambda_k2, attn_subln_g, w_attn_o, w_ssd_o, w_out, norm_mlp_g, w_mlp1, w_mlp2, final_norm_g):
    batch, n_lat, d = x.shape
    n_ctx = ctx.shape[1]
    n_tok = n_lat + n_ctx
    depth = ada_w.shape[0]
    gh = SSD_GROUP_HEADS

    o_q, o_k, o_v = 0, 1024, 2048
    o_z = 3072
    o_xbc = o_z + SSD_D_INNER
    o_dt = o_xbc + SSD_D_INNER + SSD_BC
    o_ga = o_dt + 2 * SSD_HEADS
    o_gs = o_ga + D_MODEL

    scale = ATT_HEAD_DIM ** -0.5 * math.log2(math.e)
    w_qk = jnp.concatenate([w_in[:, :, o_q:o_k] * scale, w_in[:, :, o_k:o_v]], axis=-1).astype(BF16)
    w_zvg = jnp.concatenate([w_in[:, :, o_z:o_xbc], w_in[:, :, o_v:o_z], w_in[:, :, o_ga:]], axis=-1).astype(BF16)
    z_col0, v_col0, ga_col0, gs_col0 = 0, SSD_D_INNER, SSD_D_INNER + D_MODEL, SSD_D_INNER + 2 * D_MODEL
    w_xbc = w_in[:, :, o_xbc:o_dt].astype(BF16)
    perm = jnp.asarray([f * SSD_HEADS + g * gh + h for g in range(SSD_GROUPS) for f in range(2) for h in range(gh)])
    w_dt_t = jnp.swapaxes(w_in[:, :, o_dt:o_ga][:, :, perm], 1, 2).astype(BF16)
    dt_bias = jnp.concatenate([dt_bias_f, dt_bias_b], axis=-1)[:, perm][:, :, None]
    a_log = jnp.concatenate([a_log_f, a_log_b], axis=-1)[:, perm].reshape(depth, SSD_GROUPS, 2 * gh, 1)
    a_log = jnp.broadcast_to(a_log, (depth, SSD_GROUPS, 2 * gh, SSD_CHUNK)).astype(F32)
    d_x = jnp.repeat(ssd_d, SSD_HEAD_DIM, axis=-1)[:, None, :].astype(F32)
    wa_b, ws_b, wo_b = w_attn_o.astype(BF16), w_ssd_o.astype(BF16), w_out.astype(BF16)
    w1_b, w2_b = w_mlp1.astype(BF16), w_mlp2.astype(BF16)
    lam_vecs = jnp.stack([lambda_q1, lambda_k1, lambda_q2, lambda_k2], axis=1).astype(F32)
    rope = _rope_tables(n_lat, n_ctx)

    rows_pad = -(-(batch + 1) // MOD_ROWS_PAD) * MOD_ROWS_PAD
    c_all = jnp.concatenate([c, c_ctx[None, :], jnp.zeros((rows_pad - batch - 1, d), F32)], axis=0)
    mods = _mod_call(c_all, ada_w, ada_b).reshape(depth, rows_pad, N_MOD, d)

    xt = jnp.concatenate([x, ctx], axis=1).reshape(batch * n_tok, d)
    for l in range(depth):
        lam_init = 0.8 - 0.6 * math.exp(-0.3 * l)
        h = _norm_mod_call(xt, mods[l], norm_mix_g[l][None, :], n_tok, n_lat, batch)
        qk = _proj_call(h, w_qk[l], BF16, n_tok, rope=rope)
        zvg = _proj_call(h, w_zvg[l], BF16, n_tok)
        xbc = _proj_call(h, w_xbc[l], F32, n_tok)
        dt = _dt_call(h, w_dt_t[l], dt_bias[l], n_tok, batch)
        dt = dt.reshape(batch, SSD_GROUPS, 2 * gh, n_tok)
        xs = _conv_call(xbc, conv_w[l], conv_b[l][None, :], 0, SSD_D_INNER, F32, n_tok, n_lat)
        bcm = _conv_call(xbc, conv_w[l], conv_b[l][None, :], SSD_D_INNER, SSD_BC, BF16, n_tok, n_lat)
        att = _attn_call(qk, zvg, v_col0, lam_vecs[l], attn_subln_g[l][None, :], lam_init, n_tok, n_lat, batch)
        y = _ssd_call(xs, bcm, dt, a_log[l], d_x[l], n_tok, n_lat, batch)
        xt, h2 = _mix_out_call(att, y, zvg, z_col0, ga_col0, gs_col0, xt, mods[l], ssd_norm_g[l][None, :],
                               norm_mlp_g[l][None, :], wa_b[l], ws_b[l], wo_b[l], n_tok, n_lat, batch)
        xt = _mlp_call(h2, xt, mods[l], w1_b[l], w2_b[l], n_tok, n_lat, batch)
    return _final_norm_call(xt, final_norm_g[None, :], n_tok, n_lat, batch)
```

```python
import functools
import math

import jax
import jax.numpy as jnp
from jax import lax
from jax.experimental import pallas as pl
from jax.experimental.pallas import tpu as pltpu

F32 = jnp.float32
BF16 = jnp.bfloat16

D_MODEL = 1024
GRID_W = 64
ATT_HEADS = 8
ATT_HEAD_DIM = 64
ATT_BLOCK = 2 * ATT_HEAD_DIM
ROPE_BASE = 10000.0
ROPE_PAIRS = ATT_HEAD_DIM // 4
SSD_D_INNER = 2 * D_MODEL
SSD_HEAD_DIM = 64
SSD_HEADS = SSD_D_INNER // SSD_HEAD_DIM
SSD_GROUPS = 8
SSD_GROUP_HEADS = SSD_HEADS // SSD_GROUPS
SSD_GROUP_COLS = SSD_GROUP_HEADS * SSD_HEAD_DIM
SSD_STATE = 128
SSD_CONV = 5
SSD_CHUNK = 128
SSD_BC = 2 * SSD_GROUPS * SSD_STATE
D_FF = 4 * D_MODEL
N_MOD = 6
EPS = 1e-6
CONV_HALO = 16
MOD_ROWS_PAD = 8
VMEM_LIMIT = 56 * 1024 * 1024


def _params(sem, vmem=None):
    return pltpu.CompilerParams(dimension_semantics=sem, vmem_limit_bytes=vmem)


def _pick(n, candidates):
    for c in candidates:
        if n % c == 0:
            return c
    raise ValueError(f"no tile in {candidates} divides {n}")


def _silu(v):
    return v * (1.0 / (1.0 + jnp.exp(-v)))


def _sigmoid(v):
    return 1.0 / (1.0 + jnp.exp(-v))


def _rms(v, g):
    return v * lax.rsqrt(jnp.mean(v * v, axis=-1, keepdims=True) + EPS) * g


def _is_latent(tm, tiles_per_batch, n_lat):
    off = (pl.program_id(0) % tiles_per_batch) * tm
    rows = lax.broadcasted_iota(jnp.int32, (tm, 1), 0)
    return rows < (n_lat - off)


def _mod(ml_ref, mc_ref, k, is_lat):
    return jnp.where(is_lat, ml_ref[k:k + 1, :], mc_ref[k:k + 1, :])


def _mod_kernel(c_ref, w_ref, b_ref, o_ref):
    a = _silu(c_ref[...]).astype(BF16)
    o_ref[...] = jnp.dot(a, w_ref[...].astype(BF16), preferred_element_type=F32) + b_ref[...]


def _mod_call(c_all, ada_w, ada_b):
    depth, d, nm = ada_w.shape
    rows = c_all.shape[0]
    tn = 1024
    return pl.pallas_call(
        _mod_kernel,
        grid=(depth, nm // tn),
        in_specs=[pl.BlockSpec((rows, d), lambda l, j: (0, 0)),
                  pl.BlockSpec((None, d, tn), lambda l, j: (l, 0, j)),
                  pl.BlockSpec((None, 1, tn), lambda l, j: (l, 0, j))],
        out_specs=pl.BlockSpec((None, rows, tn), lambda l, j: (l, 0, j)),
        out_shape=jax.ShapeDtypeStruct((depth, rows, nm), F32),
        compiler_params=_params(("arbitrary", "arbitrary")),
        name="adaln_mod",
    )(c_all, ada_w, ada_b.reshape(depth, 1, nm))


def _norm_mod_kernel(x_ref, ml_ref, mc_ref, g_ref, o_ref, *, tm, tpb, n_lat):
    is_lat = _is_latent(tm, tpb, n_lat)
    h = _rms(x_ref[...], g_ref[...]) * (1.0 + _mod(ml_ref, mc_ref, 1, is_lat)) + _mod(ml_ref, mc_ref, 0, is_lat)
    o_ref[...] = h.astype(o_ref.dtype)


def _norm_mod_call(x, mods, g, n_tok, n_lat, batch):
    t, d = x.shape
    tm = _pick(n_tok, (768, 384, 128))
    tpb = n_tok // tm
    return pl.pallas_call(
        functools.partial(_norm_mod_kernel, tm=tm, tpb=tpb, n_lat=n_lat),
        grid=(t // tm,),
        in_specs=[pl.BlockSpec((tm, d), lambda i: (i, 0)),
                  pl.BlockSpec((None, N_MOD, d), lambda i: (i // tpb, 0, 0)),
                  pl.BlockSpec((None, N_MOD, d), lambda i: (batch, 0, 0)),
                  pl.BlockSpec((1, d), lambda i: (0, 0))],
        out_specs=pl.BlockSpec((tm, d), lambda i: (i, 0)),
        out_shape=jax.ShapeDtypeStruct((t, d), BF16),
        compiler_params=_params(("parallel",)),
        name="norm_modulate",
    )(x, mods, mods, g)


def _proj_kernel(a_ref, w_ref, o_ref):
    o_ref[...] = jnp.dot(a_ref[...], w_ref[...], preferred_element_type=F32).astype(o_ref.dtype)


def _proj_rope_kernel(a_ref, w_ref, cos_ref, sin_ref, o_ref, *, tn):
    acc = jnp.dot(a_ref[...], w_ref[...], preferred_element_type=F32)
    cos = cos_ref[...]
    sin = sin_ref[...]
    lane = lax.broadcasted_iota(jnp.int32, cos.shape, 1)
    first_half = (lane % ATT_HEAD_DIM) < (ATT_HEAD_DIM // 2)
    for j in range(tn // ATT_BLOCK):
        blk = acc[:, j * ATT_BLOCK:(j + 1) * ATT_BLOCK]
        partner = jnp.where(first_half,
                            pltpu.roll(blk, ATT_BLOCK - ATT_HEAD_DIM // 2, 1),
                            pltpu.roll(blk, ATT_HEAD_DIM // 2, 1))
        o_ref[:, j * ATT_BLOCK:(j + 1) * ATT_BLOCK] = (blk * cos + partner * sin).astype(o_ref.dtype)


def _proj_call(h, w, out_dtype, n_tok, rope=None):
    t, k = h.shape
    n = w.shape[1]
    tm = _pick(n_tok, (1152, 384, 128))
    tpb = n_tok // tm
    tn = _pick(n, (1024, 512, 256, 128))
    in_specs = [pl.BlockSpec((tm, k), lambda i, j: (i, 0)),
                pl.BlockSpec((k, tn), lambda i, j: (0, j))]
    args = [h, w]
    if rope is None:
        body = _proj_kernel
    else:
        body = functools.partial(_proj_rope_kernel, tn=tn)
        in_specs += [pl.BlockSpec((tm, ATT_BLOCK), lambda i, j: (i % tpb, 0)),
                     pl.BlockSpec((tm, ATT_BLOCK), lambda i, j: (i % tpb, 0))]
        args += list(rope)
    return pl.pallas_call(
        body,
        grid=(t // tm, n // tn),
        in_specs=in_specs,
        out_specs=pl.BlockSpec((tm, tn), lambda i, j: (i, j)),
        out_shape=jax.ShapeDtypeStruct((t, n), out_dtype),
        compiler_params=_params(("parallel", "arbitrary"), VMEM_LIMIT),
        name="in_proj_rope" if rope is not None else "in_proj",
    )(*args)


def _dt_kernel(w_ref, h_ref, b_ref, o_ref):
    raw = lax.dot_general(w_ref[...], h_ref[...], (((1,), (1,)), ((), ())), preferred_element_type=F32)
    v = raw + b_ref[...]
    o_ref[...] = jnp.maximum(v, 0.0) + jnp.log1p(jnp.exp(-jnp.abs(v)))


def _dt_call(h, w_dt_t, dt_bias, n_tok, batch):
    t, k = h.shape
    nh = w_dt_t.shape[0]
    tm = _pick(n_tok, (1152, 384, 128))
    tpb = n_tok // tm
    return pl.pallas_call(
        _dt_kernel,
        grid=(t // tm,),
        in_specs=[pl.BlockSpec((nh, k), lambda i: (0, 0)),
                  pl.BlockSpec((tm, k), lambda i: (i, 0)),
                  pl.BlockSpec((nh, 1), lambda i: (0, 0))],
        out_specs=pl.BlockSpec((None, nh, tm), lambda i: (i // tpb, 0, i % tpb)),
        out_shape=jax.ShapeDtypeStruct((batch, nh, n_tok), F32),
        compiler_params=_params(("parallel",)),
        name="dt_proj",
    )(w_dt_t, h, dt_bias)


CONV_COLS = 256


def _proj_conv_kernel(hp_ref, h_ref, hn_ref, w_ref, cw_ref, cb_ref, o_ref, win0_ref, win1_ref, res_ref,
                      *, tm, tn, tpb, n_lat):
    t = pl.program_id(0) % tpb
    tb, lb = divmod(n_lat, tm)
    halo = CONV_HALO
    no_prev = t == 0
    no_next = t == tpb - 1
    if lb == 0:
        no_prev = jnp.logical_or(no_prev, t == tb)
        no_next = jnp.logical_or(no_next, t == tb - 1)
    half = SSD_CONV // 2

    lanes = 128
    nv = (tm + 2 * halo) // 8

    wins = (win0_ref, win1_ref)
    rel = lax.broadcasted_iota(jnp.int32, (2 * halo, 1), 0) - halo

    def project(i):
        c0 = i * CONV_COLS
        w = w_ref[:, c0:c0 + CONV_COLS]
        main = jnp.dot(h_ref[...], w, preferred_element_type=F32)
        prev = jnp.where(no_prev, 0.0, jnp.dot(hp_ref[...], w, preferred_element_type=F32))
        nxt = jnp.where(no_next, 0.0, jnp.dot(hn_ref[...], w, preferred_element_type=F32))
        for j in range(CONV_COLS // lanes):
            cj = slice(j * lanes, (j + 1) * lanes)
            wins[i % 2][j, 0:halo, :] = prev[:, cj]
            wins[i % 2][j, halo:halo + tm, :] = main[:, cj]
            wins[i % 2][j, halo + tm:, :] = nxt[:, cj]

    def convolve(i):
        win_ref = wins[i % 2]
        for j in range(CONV_COLS // lanes):
            c = slice(i * CONV_COLS + j * lanes, i * CONV_COLS + (j + 1) * lanes)
            slabs = [win_ref[j, pl.ds(v, 8, stride=nv), :] for v in range(nv)]
            cwb = [jnp.broadcast_to(cw_ref[k:k + 1, c], (8, lanes)) for k in range(SSD_CONV)]
            bias = jnp.broadcast_to(cb_ref[:, c], (8, lanes))
            for v in range(nv):
                acc = bias
                for k in range(SSD_CONV):
                    u = v + k - half
                    if u >= nv:
                        tap = pltpu.roll(slabs[u - nv], 7, 0)
                    elif u < 0:
                        tap = pltpu.roll(slabs[u + nv], 1, 0)
                    else:
                        tap = slabs[u]
                    acc = acc + tap * cwb[k]
                res_ref[j, pl.ds(v, 8, stride=nv), :] = _silu(acc)
            o_ref[:, c] = res_ref[j, halo:halo + tm, :].astype(o_ref.dtype)
            if lb:
                acc = cb_ref[:, c]
                for k in range(SSD_CONV):
                    tap = win_ref[j, lb + k - half:lb + k - half + 2 * halo, :]
                    same_side = (rel < 0) == (rel + (k - half) < 0)
                    acc = acc + jnp.where(same_side, tap, 0.0) * cw_ref[k:k + 1, c]
                fixed = jnp.where(t == tb, _silu(acc), res_ref[j, lb:lb + 2 * halo, :])
                o_ref[lb - halo:lb + halo, c] = fixed.astype(o_ref.dtype)

    chunks = tn // CONV_COLS
    project(0)
    for i in range(chunks):
        if i + 1 < chunks:
            project(i + 1)
        convolve(i)


def _proj_conv_call(h, w, conv_w, conv_b, out_dtype, n_tok, n_lat):
    t, k = h.shape
    n = w.shape[1]
    tm = _pick(n_tok, (1152, 384, 128))
    tpb = n_tok // tm
    tn = 1024
    hb = tm // CONV_HALO
    nhb = t // CONV_HALO
    return pl.pallas_call(
        functools.partial(_proj_conv_kernel, tm=tm, tn=tn, tpb=tpb, n_lat=n_lat),
        grid=(t // tm, n // tn),
        in_specs=[pl.BlockSpec((CONV_HALO, k), lambda i, j: (jnp.maximum(i * hb - 1, 0), 0)),
                  pl.BlockSpec((tm, k), lambda i, j: (i, 0)),
                  pl.BlockSpec((CONV_HALO, k), lambda i, j: (jnp.minimum((i + 1) * hb, nhb - 1), 0)),
                  pl.BlockSpec((k, tn), lambda i, j: (0, j)),
                  pl.BlockSpec((SSD_CONV, tn), lambda i, j: (0, j)),
                  pl.BlockSpec((1, tn), lambda i, j: (0, j))],
        out_specs=pl.BlockSpec((tm, tn), lambda i, j: (i, j)),
        out_shape=jax.ShapeDtypeStruct((t, n), out_dtype),
        scratch_shapes=[pltpu.VMEM((CONV_COLS // 128, tm + 2 * CONV_HALO, 128), F32)] * 3,
        compiler_params=_params(("parallel", "arbitrary"), VMEM_LIMIT),
        name="in_proj_conv",
    )(h, h, h, w, conv_w, conv_b)


ATT_GROUP = 4


def _attn_kernel(q_ref, k_ref, v_ref, lam_ref, g_ref, o_ref, *, tq, n_lat, n_tok, lam_init):
    lv = lam_ref[...]
    lam = (jnp.exp(jnp.sum(lv[0:1, :] * lv[1:2, :], axis=-1, keepdims=True))
           - jnp.exp(jnp.sum(lv[2:3, :] * lv[3:4, :], axis=-1, keepdims=True)) + lam_init)
    lane = lax.broadcasted_iota(jnp.int32, (tq, ATT_BLOCK), 1)
    nt = (((1,), (1,)), ((), ()))

    def exp_rows(s):
        e = jnp.exp2(s - jnp.max(s, axis=-1, keepdims=True))
        return e, jnp.sum(e, axis=-1, keepdims=True)

    def attend(tiles, k0, k1):
        k = k_ref[k0:k1, :]
        rows = [pl.ds(pl.multiple_of(t * tq, tq), tq) for t in tiles]
        scores = []
        for r in rows:
            q = q_ref[r, :]
            zero = jnp.zeros_like(q)
            scores.append((lax.dot_general(jnp.where(lane < ATT_HEAD_DIM, q, zero), k, nt, preferred_element_type=F32),
                           lax.dot_general(jnp.where(lane < ATT_HEAD_DIM, zero, q), k, nt, preferred_element_type=F32)))
        probs = []
        for s1, s2 in scores:
            e1, l1 = exp_rows(s1)
            e2, l2 = exp_rows(s2)
            probs.append(((e1 - (lam * l1 / l2) * e2).astype(BF16), 1.0 / l1))
        for r, (a, inv_l1) in zip(rows, probs):
            o = jnp.dot(a, v_ref[k0:k1, :], preferred_element_type=F32) * inv_l1
            o_ref[r, :] = (_rms(o, g_ref[...]) * (1.0 - lam_init)).astype(o_ref.dtype)

    lat_tiles, ctx_tiles = n_lat // tq, (n_tok - n_lat) // tq
    group = next(g for g in range(ATT_GROUP, 0, -1) if lat_tiles % g == 0)

    def latent(i, carry):
        attend([i * group + j for j in range(group)], 0, n_tok)
        return carry

    lax.fori_loop(0, lat_tiles // group, latent, 0)
    attend([lat_tiles + j for j in range(ctx_tiles)], n_lat, n_tok)


def _attn_call(qk, vzg, v_col0, lam_vecs, subln_g, lam_init, n_tok, n_lat, batch):
    t = qk.shape[0]
    tq = _pick(math.gcd(n_lat, n_tok - n_lat), (256, 128))
    vb0 = v_col0 // ATT_BLOCK
    return pl.pallas_call(
        functools.partial(_attn_kernel, tq=tq, n_lat=n_lat, n_tok=n_tok, lam_init=lam_init),
        grid=(batch, ATT_HEADS),
        in_specs=[pl.BlockSpec((n_tok, ATT_BLOCK), lambda b, h: (b, h)),
                  pl.BlockSpec((n_tok, ATT_BLOCK), lambda b, h: (b, ATT_HEADS + h)),
                  pl.BlockSpec((n_tok, ATT_BLOCK), lambda b, h: (b, vb0 + h)),
                  pl.BlockSpec((4, ATT_HEAD_DIM), lambda b, h: (0, 0)),
                  pl.BlockSpec((1, ATT_BLOCK), lambda b, h: (0, 0))],
        out_specs=pl.BlockSpec((n_tok, ATT_BLOCK), lambda b, h: (b, h)),
        out_shape=jax.ShapeDtypeStruct((t, ATT_HEADS * ATT_BLOCK), BF16),
        compiler_params=_params(("parallel", "parallel"), VMEM_LIMIT),
        name="diff_attention",
    )(qk, qk, vzg, lam_vecs, subln_g)


def _split3(v):
    hi = v.astype(BF16)
    r1 = v - hi.astype(F32)
    mid = r1.astype(BF16)
    lo = (r1 - mid.astype(F32)).astype(BF16)
    return hi, mid, lo


def _ssd_constants():
    gh, gc, n = SSD_GROUP_HEADS, SSD_GROUP_COLS, SSD_CHUNK
    j = jnp.arange(n)
    tri2 = jnp.concatenate([j[:, None] <= j[None, :], j[:, None] >= j[None, :]], axis=1)
    lane = jnp.arange(4 * gc)
    expand = j[:, None] == (2 * gh + lane // SSD_HEAD_DIM)[None, :]
    return tri2.astype(BF16), expand.astype(BF16)


def _ssd_kernel(xs_ref, b_ref, c_ref, dt_ref, alog_ref, d_ref, tri2_ref, expand_ref,
                y_ref, st_ref, sst_ref, ecs_ref, dec_ref, hin_ref, *, n_lat, n_tok):
    n_chunks = n_tok // SSD_CHUNK
    lat_chunks = n_lat // SSD_CHUNK
    ctx_chunks = n_chunks - lat_chunks
    gh, gc, n = SSD_GROUP_HEADS, SSD_GROUP_COLS, SSD_CHUNK
    a8 = -jnp.exp(alog_ref[...]) * math.log2(math.e)
    row = lax.broadcasted_iota(jnp.int32, (n, n), 0)
    col = lax.broadcasted_iota(jnp.int32, (n, n), 1)
    below, above = col < row, col > row
    fwd_rows = lax.broadcasted_iota(jnp.int32, (2 * gh, n), 0) < gh
    head_of_lane = lax.broadcasted_iota(jnp.int32, (1, gc), 1) // SSD_HEAD_DIM
    pad_rows = jnp.zeros((n - 6 * gh, n), F32)

    def mm3(v, m):
        return sum(jnp.dot(p, m, preferred_element_type=F32) for p in _split3(v))

    def rows(c):
        return pl.ds(pl.multiple_of(c * n, n), n)

    slot_of_lane = lax.broadcasted_iota(jnp.int32, (1, 2 * gc), 1) // SSD_HEAD_DIM

    def per_slot(v):
        out = v[2 * gh - 1:2 * gh, 0:1]
        for k in range(2 * gh - 2, -1, -1):
            out = jnp.where(slot_of_lane == k, v[k:k + 1, 0:1], out)
        return out

    group = next(k for k in (3, 2, 1) if n_chunks % k == 0)

    def local(i, carry):
        cs = [i * group + k for k in range(group)]
        dt8s = [dt_ref[:, rows(c)] for c in cs]
        cs2s = [mm3(dt8 * a8, tri2_ref[...]) for dt8 in dt8s]
        cs8s, colss, decs = [], [], []
        for dt8, cs2 in zip(dt8s, cs2s):
            cs8 = jnp.where(fwd_rows, cs2[:, :n], cs2[:, n:])
            tot = jnp.where(fwd_rows, cs8[:, n - 1:n], cs8[:, 0:1])
            e8 = jnp.exp2(cs8)
            w8 = dt8 * jnp.exp2(tot - cs8)
            cols = jnp.concatenate([cs8, e8, w8, pad_rows], axis=0).T
            cs8s.append(cs8)
            colss.append(cols)
            decs.append(per_slot(jnp.exp2(tot)))
        ews = [jnp.dot(cols.astype(BF16), expand_ref[...], preferred_element_type=F32) for cols in colss]
        bcs = [b_ref[rows(c), :] for c in cs]
        xcs = [xs_ref[rows(c), :] for c in cs]
        for c, ew, bc, xc, dec in zip(cs, ews, bcs, xcs, decs):
            ecs_ref[rows(c), :] = ew[:, :2 * gc]
            dec_ref[c] = jnp.broadcast_to(dec, (8, 2 * gc))
            xw = (jnp.concatenate([xc, xc], axis=1) * ew[:, 2 * gc:]).astype(BF16)
            sst_ref[c] = lax.dot_general(bc, xw, (((0,), (0,)), ((), ())), preferred_element_type=F32)
        cbs = [lax.dot_general(c_ref[rows(c), :], bc, (((1,), (1,)), ((), ())), preferred_element_type=F32)
               for c, bc in zip(cs, bcs)]
        for c, cb, cs8, cols, dt8, xc in zip(cs, cbs, cs8s, colss, dt8s, xcs):
            xb = xc.astype(BF16)
            dsum = dt8[:gh, :] + dt8[gh:, :]
            ms, xm = [], []
            for h in range(gh):
                seg = jnp.where(below, cols[:, h:h + 1] - cs8[h:h + 1, :],
                                cols[:, gh + h:gh + h + 1] - cs8[gh + h:gh + h + 1, :])
                dts = jnp.where(below, dt8[h:h + 1, :], jnp.where(above, dt8[gh + h:gh + h + 1, :], dsum[h:h + 1, :]))
                ms.append((cb * jnp.exp2(seg) * dts).astype(BF16))
                xm.append(jnp.where(head_of_lane == h, xb, jnp.zeros_like(xb)))
            y = jnp.dot(jnp.concatenate(ms, axis=1), jnp.concatenate(xm, axis=0), preferred_element_type=F32)
            y_ref[rows(c), :] = y + d_ref[...] * xc
        return carry

    lax.fori_loop(0, n_chunks // group, local, 0)

    st_ref[...] = jnp.zeros_like(st_ref)

    def scan(i, carry):
        cf = jnp.where(i < ctx_chunks, lat_chunks + i, i - ctx_chunks)
        cr = n_chunks - 1 - i
        st = st_ref[...]
        hin_ref[cf, :, :gc] = st[:, :gc].astype(BF16)
        hin_ref[cr, :, gc:] = st[:, gc:].astype(BF16)
        st_ref[:, :gc] = st[:, :gc] * dec_ref[cf][0:1, :gc] + sst_ref[cf][:, :gc]
        st_ref[:, gc:] = st[:, gc:] * dec_ref[cr][0:1, gc:] + sst_ref[cr][:, gc:]
        return carry

    lax.fori_loop(0, n_chunks, scan, 0)

    def inter(i, carry):
        cs = [i * group + k for k in range(group)]
        ts = [jnp.dot(c_ref[rows(c), :], hin_ref[c], preferred_element_type=F32) for c in cs]
        for c, t in zip(cs, ts):
            t = t * ecs_ref[rows(c), :]
            y_ref[rows(c), :] += t[:, :gc] + t[:, gc:]
        return carry

    lax.fori_loop(0, n_chunks // group, inter, 0)


def _ssd_call(xs, bcm, dt, alog, d_x, n_tok, n_lat, batch):
    t = xs.shape[0]
    gc = SSD_GROUP_COLS
    n_chunks = n_tok // SSD_CHUNK
    consts = _ssd_constants()
    const_specs = [pl.BlockSpec(m.shape, lambda b, g: (0, 0)) for m in consts]
    return pl.pallas_call(
        functools.partial(_ssd_kernel, n_lat=n_lat, n_tok=n_tok),
        grid=(batch, SSD_GROUPS),
        in_specs=[pl.BlockSpec((n_tok, gc), lambda b, g: (b, g)),
                  pl.BlockSpec((n_tok, SSD_STATE), lambda b, g: (b, g)),
                  pl.BlockSpec((n_tok, SSD_STATE), lambda b, g: (b, SSD_GROUPS + g)),
                  pl.BlockSpec((None, None, 2 * SSD_GROUP_HEADS, n_tok), lambda b, g: (b, g, 0, 0)),
                  pl.BlockSpec((None, 2 * SSD_GROUP_HEADS, SSD_CHUNK), lambda b, g: (g, 0, 0)),
                  pl.BlockSpec((1, gc), lambda b, g: (0, g))] + const_specs,
        out_specs=pl.BlockSpec((n_tok, gc), lambda b, g: (b, g)),
        out_shape=jax.ShapeDtypeStruct((t, SSD_D_INNER), F32),
        scratch_shapes=[pltpu.VMEM((SSD_STATE, 2 * gc), F32),
                        pltpu.VMEM((n_chunks, SSD_STATE, 2 * gc), F32),
                        pltpu.VMEM((n_tok, 2 * gc), F32),
                        pltpu.VMEM((n_chunks, 8, 2 * gc), F32),
                        pltpu.VMEM((n_chunks, SSD_STATE, 2 * gc), BF16)],
        compiler_params=_params(("parallel", "arbitrary"), VMEM_LIMIT),
        name="ssd_bidir",
    )(xs, bcm, bcm, dt, alog, d_x, *consts)


def _mix_out_kernel(att_ref, y_ref, z_ref, ga_ref, gs_ref, x_ref, ml_ref, mc_ref, sg_ref, ng_ref,
                    wa_ref, ws_ref, wo_ref, xo_ref, h_ref, *, tm, tpb, n_lat):
    is_lat = _is_latent(tm, tpb, n_lat)
    ssd_n = _rms(y_ref[...] * _silu(z_ref[...].astype(F32)), sg_ref[...]).astype(BF16)
    acc_s = jnp.dot(ssd_n, ws_ref[...], preferred_element_type=F32)
    acc_a = jnp.dot(att_ref[...], wa_ref[...], preferred_element_type=F32)
    merged = (_sigmoid(ga_ref[...].astype(F32)) * acc_a + _sigmoid(gs_ref[...].astype(F32)) * acc_s).astype(BF16)
    o = jnp.dot(merged, wo_ref[...], preferred_element_type=F32)
    x_new = x_ref[...] + _mod(ml_ref, mc_ref, 2, is_lat) * o
    xo_ref[...] = x_new
    h = _rms(x_new, ng_ref[...]) * (1.0 + _mod(ml_ref, mc_ref, 4, is_lat)) + _mod(ml_ref, mc_ref, 3, is_lat)
    h_ref[...] = h.astype(h_ref.dtype)


def _mix_out_call(att, y, vzg, z_col0, ga_col0, gs_col0, x, mods, ssd_g, mlp_g, wa, ws, wo, n_tok, n_lat, batch):
    t, d = x.shape
    tm = _pick(n_tok, (384, 128))
    tpb = n_tok // tm
    row = lambda i: (i, 0)
    const = lambda i: (0, 0)
    return pl.pallas_call(
        functools.partial(_mix_out_kernel, tm=tm, tpb=tpb, n_lat=n_lat),
        grid=(t // tm,),
        in_specs=[pl.BlockSpec((tm, d), row),
                  pl.BlockSpec((tm, SSD_D_INNER), row),
                  pl.BlockSpec((tm, SSD_D_INNER), lambda i: (i, z_col0 // SSD_D_INNER)),
                  pl.BlockSpec((tm, d), lambda i: (i, ga_col0 // d)),
                  pl.BlockSpec((tm, d), lambda i: (i, gs_col0 // d)),
                  pl.BlockSpec((tm, d), row),
                  pl.BlockSpec((None, N_MOD, d), lambda i: (i // tpb, 0, 0)),
                  pl.BlockSpec((None, N_MOD, d), lambda i: (batch, 0, 0)),
                  pl.BlockSpec((1, SSD_D_INNER), const),
                  pl.BlockSpec((1, d), const),
                  pl.BlockSpec((d, d), const),
                  pl.BlockSpec((SSD_D_INNER, d), const),
                  pl.BlockSpec((d, d), const)],
        out_specs=[pl.BlockSpec((tm, d), row), pl.BlockSpec((tm, d), row)],
        out_shape=[jax.ShapeDtypeStruct((t, d), F32), jax.ShapeDtypeStruct((t, d), BF16)],
        compiler_params=_params(("parallel",), VMEM_LIMIT),
        name="mixer_out",
    )(att, y, vzg, vzg, vzg, x, mods, mods, ssd_g, mlp_g, wa, ws, wo)


def _mlp_kernel(*refs, tm, tpb, n_lat, emit_next):
    if emit_next:
        h_ref, x_ref, ml_ref, mc_ref, w1_ref, w2_ref, nml_ref, nmc_ref, ng_ref, o_ref, hn_ref, acc_ref = refs
    else:
        h_ref, x_ref, ml_ref, mc_ref, w1_ref, w2_ref, o_ref, acc_ref = refs
    f = pl.program_id(1)

    @pl.when(f == 0)
    def _():
        acc_ref[...] = jnp.zeros_like(acc_ref)

    u = jnp.maximum(jnp.dot(h_ref[...], w1_ref[...], preferred_element_type=F32), 0.0)
    acc_ref[...] += jnp.dot((u * u).astype(BF16), w2_ref[...], preferred_element_type=F32)

    @pl.when(f == pl.num_programs(1) - 1)
    def _():
        is_lat = _is_latent(tm, tpb, n_lat)
        x_new = x_ref[...] + _mod(ml_ref, mc_ref, 5, is_lat) * acc_ref[...]
        o_ref[...] = x_new
        if emit_next:
            hn = _rms(x_new, ng_ref[...]) * (1.0 + _mod(nml_ref, nmc_ref, 1, is_lat)) + _mod(nml_ref, nmc_ref, 0, is_lat)
            hn_ref[...] = hn.astype(hn_ref.dtype)


def _mlp_call(h, x, mods, w1, w2, n_tok, n_lat, batch, next_mods=None, next_g=None):
    t, d = x.shape
    ff = w1.shape[1]
    tm = _pick(n_tok, (768, 384, 128))
    tpb = n_tok // tm
    tf = 512
    emit_next = next_mods is not None
    row = pl.BlockSpec((tm, d), lambda i, f: (i, 0))
    mod_l = pl.BlockSpec((None, N_MOD, d), lambda i, f: (i // tpb, 0, 0))
    mod_c = pl.BlockSpec((None, N_MOD, d), lambda i, f: (batch, 0, 0))
    in_specs = [row, row, mod_l, mod_c,
                pl.BlockSpec((d, tf), lambda i, f: (0, f)),
                pl.BlockSpec((tf, d), lambda i, f: (f, 0))]
    args = [h, x, mods, mods, w1, w2]
    out_specs, out_shape = row, jax.ShapeDtypeStruct((t, d), F32)
    if emit_next:
        in_specs += [mod_l, mod_c, pl.BlockSpec((1, d), lambda i, f: (0, 0))]
        args += [next_mods, next_mods, next_g]
        out_specs, out_shape = [row, row], [out_shape, jax.ShapeDtypeStruct((t, d), BF16)]
    return pl.pallas_call(
        functools.partial(_mlp_kernel, tm=tm, tpb=tpb, n_lat=n_lat, emit_next=emit_next),
        grid=(t // tm, ff // tf),
        in_specs=in_specs,
        out_specs=out_specs,
        out_shape=out_shape,
        scratch_shapes=[pltpu.VMEM((tm, d), F32)],
        compiler_params=_params(("parallel", "arbitrary"), VMEM_LIMIT),
        name="sq_relu_mlp",
    )(*args)


def _final_norm_kernel(x_ref, g_ref, o_ref):
    o_ref[...] = _rms(x_ref[...], g_ref[...])


def _final_norm_call(x, g, n_tok, n_lat, batch):
    d = x.shape[1]
    tr = _pick(math.gcd(n_lat, n_tok), (256, 128))
    lt = n_lat // tr
    tpb = n_tok // tr
    return pl.pallas_call(
        _final_norm_kernel,
        grid=(batch, lt),
        in_specs=[pl.BlockSpec((tr, d), lambda b, i: (b * tpb + i, 0)),
                  pl.BlockSpec((1, d), lambda b, i: (0, 0))],
        out_specs=pl.BlockSpec((None, tr, d), lambda b, i: (b, i, 0)),
        out_shape=jax.ShapeDtypeStruct((batch, n_lat, d), F32),
        compiler_params=_params(("parallel", "parallel")),
        name="final_norm",
    )(x, g)


def _rope_tables(n_lat, n_ctx):
    rows = n_lat // GRID_W
    row = jnp.broadcast_to(jnp.arange(rows)[:, None], (rows, GRID_W)).reshape(-1).astype(F32)
    col = jnp.broadcast_to(jnp.arange(GRID_W)[None, :], (rows, GRID_W)).reshape(-1).astype(F32)
    inv = jnp.float32(ROPE_BASE) ** (-jnp.arange(ROPE_PAIRS, dtype=F32) / ROPE_PAIRS)
    ang = jnp.concatenate([row[:, None] * inv, col[:, None] * inv], axis=-1)
    cos, sin = jnp.cos(ang), jnp.sin(ang)
    cos_t = jnp.concatenate([cos, cos, cos, cos], axis=-1)
    sin_t = jnp.concatenate([-sin, sin, -sin, sin], axis=-1)
    cos_t = jnp.concatenate([cos_t, jnp.ones((n_ctx, ATT_BLOCK), F32)], axis=0)
    sin_t = jnp.concatenate([sin_t, jnp.zeros((n_ctx, ATT_BLOCK), F32)], axis=0)
    return cos_t, sin_t


def kernel(x, c, ctx, c_ctx, ada_w, ada_b, norm_mix_g, w_in, conv_w, conv_b, dt_bias_f, dt_bias_b, a_log_f, a_log_b, ssd_d, ssd_norm_g, lambda_q1, lambda_k1, lambda_q2, lambda_k2, attn_subln_g, w_attn_o, w_ssd_o, w_out, norm_mlp_g, w_mlp1, w_mlp2, final_norm_g):
    batch, n_lat, d = x.shape
    n_ctx = ctx.shape[1]
    n_tok = n_lat + n_ctx
    depth = ada_w.shape[0]
    gh = SSD_GROUP_HEADS

    o_q, o_k, o_v = 0, 1024, 2048
    o_z = 3072
    o_xbc = o_z + SSD_D_INNER
    o_dt = o_xbc + SSD_D_INNER + SSD_BC
    o_ga = o_dt + 2 * SSD_HEADS
    o_gs = o_ga + D_MODEL

    scale = ATT_HEAD_DIM ** -0.5 * math.log2(math.e)
    w_qk = jnp.concatenate([w_in[:, :, o_q:o_k] * scale, w_in[:, :, o_k:o_v]], axis=-1).astype(BF16)
    w_zvg = jnp.concatenate([w_in[:, :, o_z:o_xbc], w_in[:, :, o_v:o_z], w_in[:, :, o_ga:]], axis=-1).astype(BF16)
    z_col0, v_col0, ga_col0, gs_col0 = 0, SSD_D_INNER, SSD_D_INNER + D_MODEL, SSD_D_INNER + 2 * D_MODEL
    w_xs = w_in[:, :, o_xbc:o_xbc + SSD_D_INNER].astype(BF16)
    w_bcm = w_in[:, :, o_xbc + SSD_D_INNER:o_dt].astype(BF16)
    perm = jnp.asarray([f * SSD_HEADS + g * gh + h for g in range(SSD_GROUPS) for f in range(2) for h in range(gh)])
    w_dt_t = jnp.swapaxes(w_in[:, :, o_dt:o_ga][:, :, perm], 1, 2).astype(BF16)
    dt_bias = jnp.concatenate([dt_bias_f, dt_bias_b], axis=-1)[:, perm][:, :, None]
    a_log = jnp.concatenate([a_log_f, a_log_b], axis=-1)[:, perm].reshape(depth, SSD_GROUPS, 2 * gh, 1)
    a_log = jnp.broadcast_to(a_log, (depth, SSD_GROUPS, 2 * gh, SSD_CHUNK)).astype(F32)
    d_x = jnp.repeat(ssd_d, SSD_HEAD_DIM, axis=-1)[:, None, :].astype(F32)
    wa_b, ws_b, wo_b = w_attn_o.astype(BF16), w_ssd_o.astype(BF16), w_out.astype(BF16)
    w1_b, w2_b = w_mlp1.astype(BF16), w_mlp2.astype(BF16)
    lam_vecs = jnp.stack([lambda_q1, lambda_k1, lambda_q2, lambda_k2], axis=1).astype(F32)
    rope = _rope_tables(n_lat, n_ctx)

    rows_pad = -(-(batch + 1) // MOD_ROWS_PAD) * MOD_ROWS_PAD
    c_all = jnp.concatenate([c, c_ctx[None, :], jnp.zeros((rows_pad - batch - 1, d), F32)], axis=0)
    mods = _mod_call(c_all, ada_w, ada_b).reshape(depth, rows_pad, N_MOD, d)

    xt = jnp.concatenate([x, ctx], axis=1).reshape(batch * n_tok, d)
    h = _norm_mod_call(xt, mods[0], norm_mix_g[0][None, :], n_tok, n_lat, batch)
    for l in range(depth):
        lam_init = 0.8 - 0.6 * math.exp(-0.3 * l)
        qk = _proj_call(h, w_qk[l], BF16, n_tok, rope=rope)
        zvg = _proj_call(h, w_zvg[l], BF16, n_tok)
        dt = _dt_call(h, w_dt_t[l], dt_bias[l], n_tok, batch)
        dt = dt.reshape(batch, SSD_GROUPS, 2 * gh, n_tok)
        xs = _proj_conv_call(h, w_xs[l], conv_w[l, :, :SSD_D_INNER], conv_b[l, None, :SSD_D_INNER], F32, n_tok, n_lat)
        bcm = _proj_conv_call(h, w_bcm[l], conv_w[l, :, SSD_D_INNER:], conv_b[l, None, SSD_D_INNER:], BF16, n_tok, n_lat)
        att = _attn_call(qk, zvg, v_col0, lam_vecs[l], attn_subln_g[l][None, :], lam_init, n_tok, n_lat, batch)
        y = _ssd_call(xs, bcm, dt, a_log[l], d_x[l], n_tok, n_lat, batch)
        xt, h2 = _mix_out_call(att, y, zvg, z_col0, ga_col0, gs_col0, xt, mods[l], ssd_norm_g[l][None, :],
                               norm_mlp_g[l][None, :], wa_b[l], ws_b[l], wo_b[l], n_tok, n_lat, batch)
        if l + 1 < depth:
            xt, h = _mlp_call(h2, xt, mods[l], w1_b[l], w2_b[l], n_tok, n_lat, batch,
                              next_mods=mods[l + 1], next_g=norm_mix_g[l + 1][None, :])
        else:
            xt = _mlp_call(h2, xt, mods[l], w1_b[l], w2_b[l], n_tok, n_lat, batch)
    return _final_norm_call(xt, final_norm_g[None, :], n_tok, n_lat, batch)
```

```python
import functools
import math

import jax
import jax.numpy as jnp
from jax import lax
from jax.experimental import pallas as pl
from jax.experimental.pallas import tpu as pltpu

F32 = jnp.float32
BF16 = jnp.bfloat16

D_MODEL = 1024
GRID_W = 64
ATT_HEADS = 8
ATT_HEAD_DIM = 64
ATT_BLOCK = 2 * ATT_HEAD_DIM
ROPE_BASE = 10000.0
ROPE_PAIRS = ATT_HEAD_DIM // 4
SSD_D_INNER = 2 * D_MODEL
SSD_HEAD_DIM = 64
SSD_HEADS = SSD_D_INNER // SSD_HEAD_DIM
SSD_GROUPS = 8
SSD_GROUP_HEADS = SSD_HEADS // SSD_GROUPS
SSD_GROUP_COLS = SSD_GROUP_HEADS * SSD_HEAD_DIM
SSD_STATE = 128
SSD_CONV = 5
SSD_CHUNK = 128
SSD_BC = 2 * SSD_GROUPS * SSD_STATE
SSD_STAT_ROWS = 32
D_FF = 4 * D_MODEL
N_MOD = 6
EPS = 1e-6
CONV_HALO = 16
MOD_ROWS_PAD = 8
VMEM_LIMIT = 56 * 1024 * 1024


def _params(sem, vmem=None):
    return pltpu.CompilerParams(dimension_semantics=sem, vmem_limit_bytes=vmem)


def _pick(n, candidates):
    for c in candidates:
        if n % c == 0:
            return c
    raise ValueError(f"no tile in {candidates} divides {n}")


def _silu(v):
    return v * (1.0 / (1.0 + jnp.exp(-v)))


def _sigmoid(v):
    return 1.0 / (1.0 + jnp.exp(-v))


def _rms(v, g):
    return v * lax.rsqrt(jnp.mean(v * v, axis=-1, keepdims=True) + EPS) * g


def _is_latent(tm, tiles_per_batch, n_lat, part=slice(None)):
    off = (pl.program_id(0) % tiles_per_batch) * tm
    r0, r1, _ = part.indices(tm)
    rows = lax.broadcasted_iota(jnp.int32, (r1 - r0, 1), 0) + r0
    return rows < (n_lat - off)


def _mod(ml_ref, mc_ref, k, is_lat):
    return jnp.where(is_lat, ml_ref[k:k + 1, :], mc_ref[k:k + 1, :])


def _mod_kernel(c_ref, w_ref, b_ref, o_ref):
    a = _silu(c_ref[...]).astype(BF16)
    o_ref[...] = jnp.dot(a, w_ref[...].astype(BF16), preferred_element_type=F32) + b_ref[...]


def _mod_call(c_all, ada_w, ada_b):
    depth, d, nm = ada_w.shape
    rows = c_all.shape[0]
    tn = 1024
    return pl.pallas_call(
        _mod_kernel,
        grid=(depth, nm // tn),
        in_specs=[pl.BlockSpec((rows, d), lambda l, j: (0, 0)),
                  pl.BlockSpec((None, d, tn), lambda l, j: (l, 0, j)),
                  pl.BlockSpec((None, 1, tn), lambda l, j: (l, 0, j))],
        out_specs=pl.BlockSpec((None, rows, tn), lambda l, j: (l, 0, j)),
        out_shape=jax.ShapeDtypeStruct((depth, rows, nm), F32),
        compiler_params=_params(("arbitrary", "arbitrary")),
        name="adaln_mod",
    )(c_all, ada_w, ada_b.reshape(depth, 1, nm))


def _norm_mod_kernel(x_ref, ml_ref, mc_ref, g_ref, o_ref, *, tm, tpb, n_lat):
    is_lat = _is_latent(tm, tpb, n_lat)
    h = _rms(x_ref[...], g_ref[...]) * (1.0 + _mod(ml_ref, mc_ref, 1, is_lat)) + _mod(ml_ref, mc_ref, 0, is_lat)
    o_ref[...] = h.astype(o_ref.dtype)


def _norm_mod_call(x, mods, g, n_tok, n_lat, batch):
    t, d = x.shape
    tm = _pick(n_tok, (768, 384, 128))
    tpb = n_tok // tm
    return pl.pallas_call(
        functools.partial(_norm_mod_kernel, tm=tm, tpb=tpb, n_lat=n_lat),
        grid=(t // tm,),
        in_specs=[pl.BlockSpec((tm, d), lambda i: (i, 0)),
                  pl.BlockSpec((None, N_MOD, d), lambda i: (i // tpb, 0, 0)),
                  pl.BlockSpec((None, N_MOD, d), lambda i: (batch, 0, 0)),
                  pl.BlockSpec((1, d), lambda i: (0, 0))],
        out_specs=pl.BlockSpec((tm, d), lambda i: (i, 0)),
        out_shape=jax.ShapeDtypeStruct((t, d), BF16),
        compiler_params=_params(("parallel",)),
        name="norm_modulate",
    )(x, mods, mods, g)


def _proj_kernel(a_ref, w_ref, o_ref):
    o_ref[...] = jnp.dot(a_ref[...], w_ref[...], preferred_element_type=F32).astype(o_ref.dtype)


def _proj_rope_kernel(a_ref, w_ref, cos_ref, sin_ref, o_ref, *, tn):
    acc = jnp.dot(a_ref[...], w_ref[...], preferred_element_type=F32)
    cos = cos_ref[...]
    sin = sin_ref[...]
    lane = lax.broadcasted_iota(jnp.int32, cos.shape, 1)
    first_half = (lane % ATT_HEAD_DIM) < (ATT_HEAD_DIM // 2)
    for j in range(tn // ATT_BLOCK):
        blk = acc[:, j * ATT_BLOCK:(j + 1) * ATT_BLOCK]
        partner = jnp.where(first_half,
                            pltpu.roll(blk, ATT_BLOCK - ATT_HEAD_DIM // 2, 1),
                            pltpu.roll(blk, ATT_HEAD_DIM // 2, 1))
        o_ref[:, j * ATT_BLOCK:(j + 1) * ATT_BLOCK] = (blk * cos + partner * sin).astype(o_ref.dtype)


def _proj_call(h, w, out_dtype, n_tok, rope=None):
    t, k = h.shape
    n = w.shape[1]
    tm = _pick(n_tok, (1152, 384, 128))
    tpb = n_tok // tm
    tn = _pick(n, (1024, 512, 256, 128))
    in_specs = [pl.BlockSpec((tm, k), lambda i, j: (i, 0)),
                pl.BlockSpec((k, tn), lambda i, j: (0, j))]
    args = [h, w]
    if rope is None:
        body = _proj_kernel
    else:
        body = functools.partial(_proj_rope_kernel, tn=tn)
        in_specs += [pl.BlockSpec((tm, ATT_BLOCK), lambda i, j: (i % tpb, 0)),
                     pl.BlockSpec((tm, ATT_BLOCK), lambda i, j: (i % tpb, 0))]
        args += list(rope)
    return pl.pallas_call(
        body,
        grid=(t // tm, n // tn),
        in_specs=in_specs,
        out_specs=pl.BlockSpec((tm, tn), lambda i, j: (i, j)),
        out_shape=jax.ShapeDtypeStruct((t, n), out_dtype),
        compiler_params=_params(("parallel", "arbitrary"), VMEM_LIMIT),
        name="in_proj_rope" if rope is not None else "in_proj",
    )(*args)


def _dt_kernel(w_ref, h_ref, b_ref, o_ref):
    raw = lax.dot_general(w_ref[...], h_ref[...], (((1,), (1,)), ((), ())), preferred_element_type=F32)
    v = raw + b_ref[...]
    o_ref[...] = jnp.maximum(v, 0.0) + jnp.log1p(jnp.exp(-jnp.abs(v)))


def _dt_call(h, w_dt_t, dt_bias, n_tok, batch):
    t, k = h.shape
    nh = w_dt_t.shape[0]
    tm = _pick(n_tok, (1152, 384, 128))
    tpb = n_tok // tm
    return pl.pallas_call(
        _dt_kernel,
        grid=(t // tm,),
        in_specs=[pl.BlockSpec((nh, k), lambda i: (0, 0)),
                  pl.BlockSpec((tm, k), lambda i: (i, 0)),
                  pl.BlockSpec((nh, 1), lambda i: (0, 0))],
        out_specs=pl.BlockSpec((None, nh, tm), lambda i: (i // tpb, 0, i % tpb)),
        out_shape=jax.ShapeDtypeStruct((batch, nh, n_tok), F32),
        compiler_params=_params(("parallel",)),
        name="dt_proj",
    )(w_dt_t, h, dt_bias)


CONV_COLS = 256


def _proj_conv_kernel(hp_ref, h_ref, hn_ref, w_ref, cw_ref, cb_ref, o_ref, win0_ref, win1_ref, res_ref,
                      *, tm, tn, tpb, n_lat):
    t = pl.program_id(0) % tpb
    tb, lb = divmod(n_lat, tm)
    halo = CONV_HALO
    no_prev = t == 0
    no_next = t == tpb - 1
    if lb == 0:
        no_prev = jnp.logical_or(no_prev, t == tb)
        no_next = jnp.logical_or(no_next, t == tb - 1)
    half = SSD_CONV // 2

    lanes = 128
    nv = (tm + 2 * halo) // 8

    wins = (win0_ref, win1_ref)
    rel = lax.broadcasted_iota(jnp.int32, (2 * halo, 1), 0) - halo

    def project(i):
        c0 = i * CONV_COLS
        w = w_ref[:, c0:c0 + CONV_COLS]
        main = jnp.dot(h_ref[...], w, preferred_element_type=F32)
        prev = jnp.where(no_prev, 0.0, jnp.dot(hp_ref[...], w, preferred_element_type=F32))
        nxt = jnp.where(no_next, 0.0, jnp.dot(hn_ref[...], w, preferred_element_type=F32))
        for j in range(CONV_COLS // lanes):
            cj = slice(j * lanes, (j + 1) * lanes)
            wins[i % 2][j, 0:halo, :] = prev[:, cj]
            wins[i % 2][j, halo:halo + tm, :] = main[:, cj]
            wins[i % 2][j, halo + tm:, :] = nxt[:, cj]

    def convolve(i):
        win_ref = wins[i % 2]
        for j in range(CONV_COLS // lanes):
            c = slice(i * CONV_COLS + j * lanes, i * CONV_COLS + (j + 1) * lanes)
            slabs = [win_ref[j, pl.ds(v, 8, stride=nv), :] for v in range(nv)]
            cwb = [jnp.broadcast_to(cw_ref[k:k + 1, c], (8, lanes)) for k in range(SSD_CONV)]
            bias = jnp.broadcast_to(cb_ref[:, c], (8, lanes))
            for v in range(nv):
                acc = bias
                for k in range(SSD_CONV):
                    u = v + k - half
                    if u >= nv:
                        tap = pltpu.roll(slabs[u - nv], 7, 0)
                    elif u < 0:
                        tap = pltpu.roll(slabs[u + nv], 1, 0)
                    else:
                        tap = slabs[u]
                    acc = acc + tap * cwb[k]
                res_ref[j, pl.ds(v, 8, stride=nv), :] = _silu(acc)
            o_ref[:, c] = res_ref[j, halo:halo + tm, :].astype(o_ref.dtype)
            if lb:
                acc = cb_ref[:, c]
                for k in range(SSD_CONV):
                    tap = win_ref[j, lb + k - half:lb + k - half + 2 * halo, :]
                    same_side = (rel < 0) == (rel + (k - half) < 0)
                    acc = acc + jnp.where(same_side, tap, 0.0) * cw_ref[k:k + 1, c]
                fixed = jnp.where(t == tb, _silu(acc), res_ref[j, lb:lb + 2 * halo, :])
                o_ref[lb - halo:lb + halo, c] = fixed.astype(o_ref.dtype)

    chunks = tn // CONV_COLS
    project(0)
    for i in range(chunks):
        if i + 1 < chunks:
            project(i + 1)
        convolve(i)


def _proj_conv_call(h, w, conv_w, conv_b, out_dtype, n_tok, n_lat):
    t, k = h.shape
    n = w.shape[1]
    tm = _pick(n_tok, (1152, 384, 128))
    tpb = n_tok // tm
    tn = 1024
    hb = tm // CONV_HALO
    nhb = t // CONV_HALO
    return pl.pallas_call(
        functools.partial(_proj_conv_kernel, tm=tm, tn=tn, tpb=tpb, n_lat=n_lat),
        grid=(t // tm, n // tn),
        in_specs=[pl.BlockSpec((CONV_HALO, k), lambda i, j: (jnp.maximum(i * hb - 1, 0), 0)),
                  pl.BlockSpec((tm, k), lambda i, j: (i, 0)),
                  pl.BlockSpec((CONV_HALO, k), lambda i, j: (jnp.minimum((i + 1) * hb, nhb - 1), 0)),
                  pl.BlockSpec((k, tn), lambda i, j: (0, j)),
                  pl.BlockSpec((SSD_CONV, tn), lambda i, j: (0, j)),
                  pl.BlockSpec((1, tn), lambda i, j: (0, j))],
        out_specs=pl.BlockSpec((tm, tn), lambda i, j: (i, j)),
        out_shape=jax.ShapeDtypeStruct((t, n), out_dtype),
        scratch_shapes=[pltpu.VMEM((CONV_COLS // 128, tm + 2 * CONV_HALO, 128), F32)] * 3,
        compiler_params=_params(("parallel", "arbitrary"), VMEM_LIMIT),
        name="in_proj_conv",
    )(h, h, h, w, conv_w, conv_b)


ATT_GROUP = 8


def _attn_kernel(q_ref, k_ref, v_ref, lam_ref, g_ref, o_ref, *, tq, n_lat, n_tok, lam_init):
    lv = lam_ref[...]
    lam = (jnp.exp(jnp.sum(lv[0:1, :] * lv[1:2, :], axis=-1, keepdims=True))
           - jnp.exp(jnp.sum(lv[2:3, :] * lv[3:4, :], axis=-1, keepdims=True)) + lam_init)
    lane = lax.broadcasted_iota(jnp.int32, (tq, ATT_BLOCK), 1)
    nt = (((1,), (1,)), ((), ()))

    def exp_rows(s):
        e = jnp.exp2(s - jnp.max(s, axis=-1, keepdims=True))
        return e, jnp.sum(e, axis=-1, keepdims=True)

    def attend(tiles, k0, k1):
        k = k_ref[k0:k1, :]
        rows = [pl.ds(pl.multiple_of(t * tq, tq), tq) for t in tiles]
        scores = []
        for r in rows:
            q = q_ref[r, :]
            zero = jnp.zeros_like(q)
            scores.append((lax.dot_general(jnp.where(lane < ATT_HEAD_DIM, q, zero), k, nt, preferred_element_type=F32),
                           lax.dot_general(jnp.where(lane < ATT_HEAD_DIM, zero, q), k, nt, preferred_element_type=F32)))
        probs = []
        for s1, s2 in scores:
            e1, l1 = exp_rows(s1)
            e2, l2 = exp_rows(s2)
            probs.append(((e1 - (lam * l1 / l2) * e2).astype(BF16), 1.0 / l1))
        for r, (a, inv_l1) in zip(rows, probs):
            o = jnp.dot(a, v_ref[k0:k1, :], preferred_element_type=F32) * inv_l1
            o_ref[r, :] = (_rms(o, g_ref[...]) * (1.0 - lam_init)).astype(o_ref.dtype)

    lat_tiles, ctx_tiles = n_lat // tq, (n_tok - n_lat) // tq
    group = next(g for g in range(ATT_GROUP, 0, -1) if lat_tiles % g == 0)

    def latent(i, carry):
        attend([i * group + j for j in range(group)], 0, n_tok)
        return carry

    lax.fori_loop(0, lat_tiles // group, latent, 0)
    attend([lat_tiles + j for j in range(ctx_tiles)], n_lat, n_tok)


def _attn_call(qk, vzg, v_col0, lam_vecs, subln_g, lam_init, n_tok, n_lat, batch):
    t = qk.shape[0]
    tq = _pick(math.gcd(n_lat, n_tok - n_lat), (256, 128))
    vb0 = v_col0 // ATT_BLOCK
    return pl.pallas_call(
        functools.partial(_attn_kernel, tq=tq, n_lat=n_lat, n_tok=n_tok, lam_init=lam_init),
        grid=(batch, ATT_HEADS),
        in_specs=[pl.BlockSpec((n_tok, ATT_BLOCK), lambda b, h: (b, h)),
                  pl.BlockSpec((n_tok, ATT_BLOCK), lambda b, h: (b, ATT_HEADS + h)),
                  pl.BlockSpec((n_tok, ATT_BLOCK), lambda b, h: (b, vb0 + h)),
                  pl.BlockSpec((4, ATT_HEAD_DIM), lambda b, h: (0, 0)),
                  pl.BlockSpec((1, ATT_BLOCK), lambda b, h: (0, 0))],
        out_specs=pl.BlockSpec((n_tok, ATT_BLOCK), lambda b, h: (b, h)),
        out_shape=jax.ShapeDtypeStruct((t, ATT_HEADS * ATT_BLOCK), BF16),
        compiler_params=_params(("parallel", "parallel"), VMEM_LIMIT),
        name="diff_attention",
    )(qk, qk, vzg, lam_vecs, subln_g)


def _split3(v):
    hi = v.astype(BF16)
    r1 = v - hi.astype(F32)
    mid = r1.astype(BF16)
    lo = (r1 - mid.astype(F32)).astype(BF16)
    return hi, mid, lo


def _ssd_constants():
    gh, gc, n = SSD_GROUP_HEADS, SSD_GROUP_COLS, SSD_CHUNK
    j = jnp.arange(n)
    tri2 = jnp.concatenate([j[:, None] <= j[None, :], j[:, None] >= j[None, :]], axis=1)
    lane = jnp.arange(4 * gc)
    expand = jnp.arange(SSD_STAT_ROWS)[:, None] == (2 * gh + lane // SSD_HEAD_DIM)[None, :]
    return tri2.astype(BF16), expand.astype(BF16)


def _ssd_kernel(xs_ref, b_ref, c_ref, dt_ref, alog_ref, d_ref, tri2_ref, expand_ref,
                y_ref, st_ref, sst_ref, ecs_ref, dec_ref, hin_ref, *, n_lat, n_tok):
    n_chunks = n_tok // SSD_CHUNK
    lat_chunks = n_lat // SSD_CHUNK
    ctx_chunks = n_chunks - lat_chunks
    gh, gc, n = SSD_GROUP_HEADS, SSD_GROUP_COLS, SSD_CHUNK
    a8 = -jnp.exp(alog_ref[...]) * math.log2(math.e)
    row = lax.broadcasted_iota(jnp.int32, (n, n), 0)
    col = lax.broadcasted_iota(jnp.int32, (n, n), 1)
    below, above = col < row, col > row
    fwd_rows = lax.broadcasted_iota(jnp.int32, (2 * gh, n), 0) < gh
    head_of_lane = lax.broadcasted_iota(jnp.int32, (1, gc), 1) // SSD_HEAD_DIM
    pad_rows = jnp.zeros((SSD_STAT_ROWS - 6 * gh, n), F32)

    def mm3(v, m):
        return sum(jnp.dot(p, m, preferred_element_type=F32) for p in _split3(v))

    def rows(c):
        return pl.ds(pl.multiple_of(c * n, n), n)

    slot_of_lane = lax.broadcasted_iota(jnp.int32, (1, 2 * gc), 1) // SSD_HEAD_DIM

    def per_slot(v):
        out = v[2 * gh - 1:2 * gh, 0:1]
        for k in range(2 * gh - 2, -1, -1):
            out = jnp.where(slot_of_lane == k, v[k:k + 1, 0:1], out)
        return out

    group = next(k for k in (9, 6, 3, 2, 1) if n_chunks % k == 0)

    def local(i, carry):
        cs = [i * group + k for k in range(group)]
        dt8s = [dt_ref[:, rows(c)] for c in cs]
        cs2s = [mm3(dt8 * a8, tri2_ref[...]) for dt8 in dt8s]
        cs8s, colss, decs = [], [], []
        for dt8, cs2 in zip(dt8s, cs2s):
            cs8 = jnp.where(fwd_rows, cs2[:, :n], cs2[:, n:])
            tot = jnp.where(fwd_rows, cs8[:, n - 1:n], cs8[:, 0:1])
            e8 = jnp.exp2(cs8)
            w8 = dt8 * jnp.exp2(tot - cs8)
            cols = jnp.concatenate([cs8, e8, w8, pad_rows], axis=0).T
            cs8s.append(cs8)
            colss.append(cols)
            decs.append(per_slot(jnp.exp2(tot)))
        ews = [jnp.dot(cols.astype(BF16), expand_ref[...], preferred_element_type=F32) for cols in colss]
        bcs = [b_ref[rows(c), :] for c in cs]
        xcs = [xs_ref[rows(c), :] for c in cs]
        for c, ew, bc, xc, dec in zip(cs, ews, bcs, xcs, decs):
            ecs_ref[rows(c), :] = ew[:, :2 * gc]
            dec_ref[c] = jnp.broadcast_to(dec, (8, 2 * gc))
            xw = (jnp.concatenate([xc, xc], axis=1) * ew[:, 2 * gc:]).astype(BF16)
            sst_ref[c] = lax.dot_general(bc, xw, (((0,), (0,)), ((), ())), preferred_element_type=F32)
        cbs = [lax.dot_general(c_ref[rows(c), :], bc, (((1,), (1,)), ((), ())), preferred_element_type=F32)
               for c, bc in zip(cs, bcs)]
        for c, cb, cs8, cols, dt8, xc in zip(cs, cbs, cs8s, colss, dt8s, xcs):
            xb = xc.astype(BF16)
            dsum = dt8[:gh, :] + dt8[gh:, :]
            ms, xm = [], []
            for h in range(gh):
                seg = jnp.where(below, cols[:, h:h + 1] - cs8[h:h + 1, :],
                                cols[:, gh + h:gh + h + 1] - cs8[gh + h:gh + h + 1, :])
                dts = jnp.where(below, dt8[h:h + 1, :], jnp.where(above, dt8[gh + h:gh + h + 1, :], dsum[h:h + 1, :]))
                ms.append((cb * jnp.exp2(seg) * dts).astype(BF16))
                xm.append(jnp.where(head_of_lane == h, xb, jnp.zeros_like(xb)))
            y = jnp.dot(jnp.concatenate(ms, axis=1), jnp.concatenate(xm, axis=0), preferred_element_type=F32)
            y_ref[rows(c), :] = y + d_ref[...] * xc
        return carry

    lax.fori_loop(0, n_chunks // group, local, 0)

    st_ref[...] = jnp.zeros_like(st_ref)

    def scan(i, carry):
        cf = jnp.where(i < ctx_chunks, lat_chunks + i, i - ctx_chunks)
        cr = n_chunks - 1 - i
        st = st_ref[...]
        hin_ref[cf, :, :gc] = st[:, :gc].astype(BF16)
        hin_ref[cr, :, gc:] = st[:, gc:].astype(BF16)
        st_ref[:, :gc] = st[:, :gc] * dec_ref[cf][0:1, :gc] + sst_ref[cf][:, :gc]
        st_ref[:, gc:] = st[:, gc:] * dec_ref[cr][0:1, gc:] + sst_ref[cr][:, gc:]
        return carry

    lax.fori_loop(0, n_chunks, scan, 0)

    def inter(i, carry):
        cs = [i * group + k for k in range(group)]
        ts = [jnp.dot(c_ref[rows(c), :], hin_ref[c], preferred_element_type=F32) for c in cs]
        for c, t in zip(cs, ts):
            t = t * ecs_ref[rows(c), :]
            y_ref[rows(c), :] += t[:, :gc] + t[:, gc:]
        return carry

    lax.fori_loop(0, n_chunks // group, inter, 0)


def _ssd_call(xs, bcm, dt, alog, d_x, n_tok, n_lat, batch):
    t = xs.shape[0]
    gc = SSD_GROUP_COLS
    n_chunks = n_tok // SSD_CHUNK
    consts = _ssd_constants()
    const_specs = [pl.BlockSpec(m.shape, lambda b, g: (0, 0)) for m in consts]
    return pl.pallas_call(
        functools.partial(_ssd_kernel, n_lat=n_lat, n_tok=n_tok),
        grid=(batch, SSD_GROUPS),
        in_specs=[pl.BlockSpec((n_tok, gc), lambda b, g: (b, g)),
                  pl.BlockSpec((n_tok, SSD_STATE), lambda b, g: (b, g)),
                  pl.BlockSpec((n_tok, SSD_STATE), lambda b, g: (b, SSD_GROUPS + g)),
                  pl.BlockSpec((None, None, 2 * SSD_GROUP_HEADS, n_tok), lambda b, g: (b, g, 0, 0)),
                  pl.BlockSpec((None, 2 * SSD_GROUP_HEADS, SSD_CHUNK), lambda b, g: (g, 0, 0)),
                  pl.BlockSpec((1, gc), lambda b, g: (0, g))] + const_specs,
        out_specs=pl.BlockSpec((n_tok, gc), lambda b, g: (b, g)),
        out_shape=jax.ShapeDtypeStruct((t, SSD_D_INNER), F32),
        scratch_shapes=[pltpu.VMEM((SSD_STATE, 2 * gc), F32),
                        pltpu.VMEM((n_chunks, SSD_STATE, 2 * gc), F32),
                        pltpu.VMEM((n_tok, 2 * gc), F32),
                        pltpu.VMEM((n_chunks, 8, 2 * gc), F32),
                        pltpu.VMEM((n_chunks, SSD_STATE, 2 * gc), BF16)],
        compiler_params=_params(("parallel", "arbitrary"), VMEM_LIMIT),
        name="ssd_bidir",
    )(xs, bcm, bcm, dt, alog, d_x, *consts)


MIX_PARTS = 2


def _mix_out_kernel(att_ref, y_ref, z_ref, ga_ref, gs_ref, x_ref, ml_ref, mc_ref, sg_ref, ng_ref,
                    wa_ref, ws_ref, wo_ref, xo_ref, h_ref, *, tm, tpb, n_lat):
    parts = [slice(p * tm // MIX_PARTS, (p + 1) * tm // MIX_PARTS) for p in range(MIX_PARTS)]
    ssd_n = [_rms(y_ref[r, :] * _silu(z_ref[r, :].astype(F32)), sg_ref[...]).astype(BF16) for r in parts]
    acc_a = [jnp.dot(att_ref[r, :], wa_ref[...], preferred_element_type=F32) for r in parts]
    acc_s = [jnp.dot(s, ws_ref[...], preferred_element_type=F32) for s in ssd_n]
    merged = [(_sigmoid(ga_ref[r, :].astype(F32)) * a + _sigmoid(gs_ref[r, :].astype(F32)) * s).astype(BF16)
              for r, a, s in zip(parts, acc_a, acc_s)]
    outs = [jnp.dot(m, wo_ref[...], preferred_element_type=F32) for m in merged]
    for r, o in zip(parts, outs):
        is_lat = _is_latent(tm, tpb, n_lat, r)
        x_new = x_ref[r, :] + _mod(ml_ref, mc_ref, 2, is_lat) * o
        xo_ref[r, :] = x_new
        h = _rms(x_new, ng_ref[...]) * (1.0 + _mod(ml_ref, mc_ref, 4, is_lat)) + _mod(ml_ref, mc_ref, 3, is_lat)
        h_ref[r, :] = h.astype(h_ref.dtype)


def _mix_out_call(att, y, vzg, z_col0, ga_col0, gs_col0, x, mods, ssd_g, mlp_g, wa, ws, wo, n_tok, n_lat, batch):
    t, d = x.shape
    tm = _pick(n_tok, (384, 128))
    tpb = n_tok // tm
    row = lambda i: (i, 0)
    const = lambda i: (0, 0)
    return pl.pallas_call(
        functools.partial(_mix_out_kernel, tm=tm, tpb=tpb, n_lat=n_lat),
        grid=(t // tm,),
        in_specs=[pl.BlockSpec((tm, d), row),
                  pl.BlockSpec((tm, SSD_D_INNER), row),
                  pl.BlockSpec((tm, SSD_D_INNER), lambda i: (i, z_col0 // SSD_D_INNER)),
                  pl.BlockSpec((tm, d), lambda i: (i, ga_col0 // d)),
                  pl.BlockSpec((tm, d), lambda i: (i, gs_col0 // d)),
                  pl.BlockSpec((tm, d), row),
                  pl.BlockSpec((None, N_MOD, d), lambda i: (i // tpb, 0, 0)),
                  pl.BlockSpec((None, N_MOD, d), lambda i: (batch, 0, 0)),
                  pl.BlockSpec((1, SSD_D_INNER), const),
                  pl.BlockSpec((1, d), const),
                  pl.BlockSpec((d, d), const),
                  pl.BlockSpec((SSD_D_INNER, d), const),
                  pl.BlockSpec((d, d), const)],
        out_specs=[pl.BlockSpec((tm, d), row), pl.BlockSpec((tm, d), row)],
        out_shape=[jax.ShapeDtypeStruct((t, d), F32), jax.ShapeDtypeStruct((t, d), BF16)],
        compiler_params=_params(("parallel",), VMEM_LIMIT),
        name="mixer_out",
    )(att, y, vzg, vzg, vzg, x, mods, mods, ssd_g, mlp_g, wa, ws, wo)


def _mlp_kernel(*refs, tm, tpb, n_lat, emit_next):
    if emit_next:
        h_ref, x_ref, ml_ref, mc_ref, w1_ref, w2_ref, nml_ref, nmc_ref, ng_ref, o_ref, hn_ref, acc_ref = refs
    else:
        h_ref, x_ref, ml_ref, mc_ref, w1_ref, w2_ref, o_ref, acc_ref = refs
    f = pl.program_id(1)

    @pl.when(f == 0)
    def _():
        acc_ref[...] = jnp.zeros_like(acc_ref)

    u = jnp.maximum(jnp.dot(h_ref[...], w1_ref[...], preferred_element_type=F32), 0.0)
    acc_ref[...] += jnp.dot((u * u).astype(BF16), w2_ref[...], preferred_element_type=F32)

    @pl.when(f == pl.num_programs(1) - 1)
    def _():
        is_lat = _is_latent(tm, tpb, n_lat)
        x_new = x_ref[...] + _mod(ml_ref, mc_ref, 5, is_lat) * acc_ref[...]
        o_ref[...] = x_new
        if emit_next:
            hn = _rms(x_new, ng_ref[...]) * (1.0 + _mod(nml_ref, nmc_ref, 1, is_lat)) + _mod(nml_ref, nmc_ref, 0, is_lat)
            hn_ref[...] = hn.astype(hn_ref.dtype)


def _mlp_call(h, x, mods, w1, w2, n_tok, n_lat, batch, next_mods=None, next_g=None):
    t, d = x.shape
    ff = w1.shape[1]
    tm = _pick(n_tok, (768, 384, 128))
    tpb = n_tok // tm
    tf = 1024
    emit_next = next_mods is not None
    row = pl.BlockSpec((tm, d), lambda i, f: (i, 0))
    mod_l = pl.BlockSpec((None, N_MOD, d), lambda i, f: (i // tpb, 0, 0))
    mod_c = pl.BlockSpec((None, N_MOD, d), lambda i, f: (batch, 0, 0))
    in_specs = [row, row, mod_l, mod_c,
                pl.BlockSpec((d, tf), lambda i, f: (0, f)),
                pl.BlockSpec((tf, d), lambda i, f: (f, 0))]
    args = [h, x, mods, mods, w1, w2]
    out_specs, out_shape = row, jax.ShapeDtypeStruct((t, d), F32)
    if emit_next:
        in_specs += [mod_l, mod_c, pl.BlockSpec((1, d), lambda i, f: (0, 0))]
        args += [next_mods, next_mods, next_g]
        out_specs, out_shape = [row, row], [out_shape, jax.ShapeDtypeStruct((t, d), BF16)]
    return pl.pallas_call(
        functools.partial(_mlp_kernel, tm=tm, tpb=tpb, n_lat=n_lat, emit_next=emit_next),
        grid=(t // tm, ff // tf),
        in_specs=in_specs,
        out_specs=out_specs,
        out_shape=out_shape,
        scratch_shapes=[pltpu.VMEM((tm, d), F32)],
        compiler_params=_params(("parallel", "arbitrary"), VMEM_LIMIT),
        name="sq_relu_mlp",
    )(*args)


def _final_norm_kernel(x_ref, g_ref, o_ref):
    o_ref[...] = _rms(x_ref[...], g_ref[...])


def _final_norm_call(x, g, n_tok, n_lat, batch):
    d = x.shape[1]
    tr = _pick(math.gcd(n_lat, n_tok), (256, 128))
    lt = n_lat // tr
    tpb = n_tok // tr
    return pl.pallas_call(
        _final_norm_kernel,
        grid=(batch, lt),
        in_specs=[pl.BlockSpec((tr, d), lambda b, i: (b * tpb + i, 0)),
                  pl.BlockSpec((1, d), lambda b, i: (0, 0))],
        out_specs=pl.BlockSpec((None, tr, d), lambda b, i: (b, i, 0)),
        out_shape=jax.ShapeDtypeStruct((batch, n_lat, d), F32),
        compiler_params=_params(("parallel", "parallel")),
        name="final_norm",
    )(x, g)


def _rope_tables(n_lat, n_ctx):
    rows = n_lat // GRID_W
    row = jnp.broadcast_to(jnp.arange(rows)[:, None], (rows, GRID_W)).reshape(-1).astype(F32)
    col = jnp.broadcast_to(jnp.arange(GRID_W)[None, :], (rows, GRID_W)).reshape(-1).astype(F32)
    inv = jnp.float32(ROPE_BASE) ** (-jnp.arange(ROPE_PAIRS, dtype=F32) / ROPE_PAIRS)
    ang = jnp.concatenate([row[:, None] * inv, col[:, None] * inv], axis=-1)
    cos, sin = jnp.cos(ang), jnp.sin(ang)
    cos_t = jnp.concatenate([cos, cos, cos, cos], axis=-1)
    sin_t = jnp.concatenate([-sin, sin, -sin, sin], axis=-1)
    cos_t = jnp.concatenate([cos_t, jnp.ones((n_ctx, ATT_BLOCK), F32)], axis=0)
    sin_t = jnp.concatenate([sin_t, jnp.zeros((n_ctx, ATT_BLOCK), F32)], axis=0)
    return cos_t, sin_t


def kernel(x, c, ctx, c_ctx, ada_w, ada_b, norm_mix_g, w_in, conv_w, conv_b, dt_bias_f, dt_bias_b, a_log_f, a_log_b, ssd_d, ssd_norm_g, lambda_q1, lambda_k1, lambda_q2, lambda_k2, attn_subln_g, w_attn_o, w_ssd_o, w_out, norm_mlp_g, w_mlp1, w_mlp2, final_norm_g):
    batch, n_lat, d = x.shape
    n_ctx = ctx.shape[1]
    n_tok = n_lat + n_ctx
    depth = ada_w.shape[0]
    gh = SSD_GROUP_HEADS

    o_q, o_k, o_v = 0, 1024, 2048
    o_z = 3072
    o_xbc = o_z + SSD_D_INNER
    o_dt = o_xbc + SSD_D_INNER + SSD_BC
    o_ga = o_dt + 2 * SSD_HEADS
    o_gs = o_ga + D_MODEL

    scale = ATT_HEAD_DIM ** -0.5 * math.log2(math.e)
    w_qk = jnp.concatenate([w_in[:, :, o_q:o_k] * scale, w_in[:, :, o_k:o_v]], axis=-1).astype(BF16)
    w_zvg = jnp.concatenate([w_in[:, :, o_z:o_xbc], w_in[:, :, o_v:o_z], w_in[:, :, o_ga:]], axis=-1).astype(BF16)
    z_col0, v_col0, ga_col0, gs_col0 = 0, SSD_D_INNER, SSD_D_INNER + D_MODEL, SSD_D_INNER + 2 * D_MODEL
    w_xs = w_in[:, :, o_xbc:o_xbc + SSD_D_INNER].astype(BF16)
    w_bcm = w_in[:, :, o_xbc + SSD_D_INNER:o_dt].astype(BF16)
    perm = jnp.asarray([f * SSD_HEADS + g * gh + h for g in range(SSD_GROUPS) for f in range(2) for h in range(gh)])
    w_dt_t = jnp.swapaxes(w_in[:, :, o_dt:o_ga][:, :, perm], 1, 2).astype(BF16)
    dt_bias = jnp.concatenate([dt_bias_f, dt_bias_b], axis=-1)[:, perm][:, :, None]
    a_log = jnp.concatenate([a_log_f, a_log_b], axis=-1)[:, perm].reshape(depth, SSD_GROUPS, 2 * gh, 1)
    a_log = jnp.broadcast_to(a_log, (depth, SSD_GROUPS, 2 * gh, SSD_CHUNK)).astype(F32)
    d_x = jnp.repeat(ssd_d, SSD_HEAD_DIM, axis=-1)[:, None, :].astype(F32)
    wa_b, ws_b, wo_b = w_attn_o.astype(BF16), w_ssd_o.astype(BF16), w_out.astype(BF16)
    w1_b, w2_b = w_mlp1.astype(BF16), w_mlp2.astype(BF16)
    lam_vecs = jnp.stack([lambda_q1, lambda_k1, lambda_q2, lambda_k2], axis=1).astype(F32)
    rope = _rope_tables(n_lat, n_ctx)

    rows_pad = -(-(batch + 1) // MOD_ROWS_PAD) * MOD_ROWS_PAD
    c_all = jnp.concatenate([c, c_ctx[None, :], jnp.zeros((rows_pad - batch - 1, d), F32)], axis=0)
    mods = _mod_call(c_all, ada_w, ada_b).reshape(depth, rows_pad, N_MOD, d)

    xt = jnp.concatenate([x, ctx], axis=1).reshape(batch * n_tok, d)
    h = _norm_mod_call(xt, mods[0], norm_mix_g[0][None, :], n_tok, n_lat, batch)
    for l in range(depth):
        lam_init = 0.8 - 0.6 * math.exp(-0.3 * l)
        qk = _proj_call(h, w_qk[l], BF16, n_tok, rope=rope)
        zvg = _proj_call(h, w_zvg[l], BF16, n_tok)
        dt = _dt_call(h, w_dt_t[l], dt_bias[l], n_tok, batch)
        dt = dt.reshape(batch, SSD_GROUPS, 2 * gh, n_tok)
        xs = _proj_conv_call(h, w_xs[l], conv_w[l, :, :SSD_D_INNER], conv_b[l, None, :SSD_D_INNER], F32, n_tok, n_lat)
        bcm = _proj_conv_call(h, w_bcm[l], conv_w[l, :, SSD_D_INNER:], conv_b[l, None, SSD_D_INNER:], BF16, n_tok, n_lat)
        att = _attn_call(qk, zvg, v_col0, lam_vecs[l], attn_subln_g[l][None, :], lam_init, n_tok, n_lat, batch)
        y = _ssd_call(xs, bcm, dt, a_log[l], d_x[l], n_tok, n_lat, batch)
        xt, h2 = _mix_out_call(att, y, zvg, z_col0, ga_col0, gs_col0, xt, mods[l], ssd_norm_g[l][None, :],
                               norm_mlp_g[l][None, :], wa_b[l], ws_b[l], wo_b[l], n_tok, n_lat, batch)
        if l + 1 < depth:
            xt, h = _mlp_call(h2, xt, mods[l], w1_b[l], w2_b[l], n_tok, n_lat, batch,
                              next_mods=mods[l + 1], next_g=norm_mix_g[l + 1][None, :])
        else:
            xt = _mlp_call(h2, xt, mods[l], w1_b[l], w2_b[l], n_tok, n_lat, batch)
    return _final_norm_call(xt, final_norm_g[None, :], n_tok, n_lat, batch)
```

```python
import functools
import math

import jax
import jax.numpy as jnp
from jax import lax
from jax.experimental import pallas as pl
from jax.experimental.pallas import tpu as pltpu

F32 = jnp.float32
BF16 = jnp.bfloat16

D_MODEL = 1024
GRID_W = 64
ATT_HEADS = 8
ATT_HEAD_DIM = 64
ATT_BLOCK = 2 * ATT_HEAD_DIM
ROPE_BASE = 10000.0
ROPE_PAIRS = ATT_HEAD_DIM // 4
SSD_D_INNER = 2 * D_MODEL
SSD_HEAD_DIM = 64
SSD_HEADS = SSD_D_INNER // SSD_HEAD_DIM
SSD_GROUPS = 8
SSD_GROUP_HEADS = SSD_HEADS // SSD_GROUPS
SSD_GROUP_COLS = SSD_GROUP_HEADS * SSD_HEAD_DIM
SSD_STATE = 128
SSD_CONV = 5
SSD_CHUNK = 128
SSD_BC = 2 * SSD_GROUPS * SSD_STATE
SSD_STAT_ROWS = 32
D_FF = 4 * D_MODEL
N_MOD = 6
EPS = 1e-6
CONV_HALO = 16
MOD_ROWS_PAD = 8
VMEM_LIMIT = 56 * 1024 * 1024


def _params(sem, vmem=None):
    return pltpu.CompilerParams(dimension_semantics=sem, vmem_limit_bytes=vmem)


def _pick(n, candidates):
    for c in candidates:
        if n % c == 0:
            return c
    raise ValueError(f"no tile in {candidates} divides {n}")


def _silu(v):
    h = 0.5 * v
    return h + h * jnp.tanh(h)


def _sigmoid(v):
    return 0.5 * jnp.tanh(0.5 * v) + 0.5


def _rms(v, g):
    return v * lax.rsqrt(jnp.mean(v * v, axis=-1, keepdims=True) + EPS) * g


def _is_latent(tm, tiles_per_batch, n_lat, part=slice(None)):
    off = (pl.program_id(0) % tiles_per_batch) * tm
    r0, r1, _ = part.indices(tm)
    rows = lax.broadcasted_iota(jnp.int32, (r1 - r0, 1), 0) + r0
    return rows < (n_lat - off)


def _mod(ml_ref, mc_ref, k, is_lat):
    return jnp.where(is_lat, ml_ref[k:k + 1, :], mc_ref[k:k + 1, :])


def _mod_kernel(c_ref, w_ref, b_ref, o_ref):
    a = _silu(c_ref[...]).astype(BF16)
    o_ref[...] = jnp.dot(a, w_ref[...].astype(BF16), preferred_element_type=F32) + b_ref[...]


def _mod_call(c_all, ada_w, ada_b):
    depth, d, nm = ada_w.shape
    rows = c_all.shape[0]
    tn = 1024
    return pl.pallas_call(
        _mod_kernel,
        grid=(depth, nm // tn),
        in_specs=[pl.BlockSpec((rows, d), lambda l, j: (0, 0)),
                  pl.BlockSpec((None, d, tn), lambda l, j: (l, 0, j)),
                  pl.BlockSpec((None, 1, tn), lambda l, j: (l, 0, j))],
        out_specs=pl.BlockSpec((None, rows, tn), lambda l, j: (l, 0, j)),
        out_shape=jax.ShapeDtypeStruct((depth, rows, nm), F32),
        compiler_params=_params(("arbitrary", "arbitrary")),
        name="adaln_mod",
    )(c_all, ada_w, ada_b.reshape(depth, 1, nm))


def _embed_kernel(x_ref, ctx_ref, ml_ref, mc_ref, g_ref, xt_ref, h_ref, *, lat_tiles):
    is_lat = pl.program_id(1) < lat_tiles
    x = jnp.where(is_lat, x_ref[...], ctx_ref[...])
    m = jnp.where(is_lat, ml_ref[...], mc_ref[...])
    xt_ref[...] = x
    h_ref[...] = (_rms(x, g_ref[...]) * (1.0 + m[1:2, :]) + m[0:1, :]).astype(h_ref.dtype)


def _embed_call(x, ctx, mods, g):
    batch, n_lat, d = x.shape
    n_ctx = ctx.shape[1]
    n_tok = n_lat + n_ctx
    tr = _pick(math.gcd(n_lat, n_ctx), (256, 128))
    lt, tpb = n_lat // tr, n_tok // tr
    row = pl.BlockSpec((tr, d), lambda b, i: (b * tpb + i, 0))
    return pl.pallas_call(
        functools.partial(_embed_kernel, lat_tiles=lt),
        grid=(batch, tpb),
        in_specs=[pl.BlockSpec((None, tr, d), lambda b, i: (b, jnp.minimum(i, lt - 1), 0)),
                  pl.BlockSpec((None, tr, d), lambda b, i: (b, jnp.maximum(i - lt, 0), 0)),
                  pl.BlockSpec((None, N_MOD, d), lambda b, i: (b, 0, 0)),
                  pl.BlockSpec((None, N_MOD, d), lambda b, i: (batch, 0, 0)),
                  pl.BlockSpec((1, d), lambda b, i: (0, 0))],
        out_specs=[row, row],
        out_shape=[jax.ShapeDtypeStruct((batch * n_tok, d), F32), jax.ShapeDtypeStruct((batch * n_tok, d), BF16)],
        compiler_params=_params(("parallel", "arbitrary")),
        name="embed_norm_modulate",
    )(x, ctx, mods, mods, g)


def _proj_kernel(a_ref, w_ref, o_ref):
    o_ref[...] = jnp.dot(a_ref[...], w_ref[...], preferred_element_type=F32).astype(o_ref.dtype)


def _proj_rope_kernel(a_ref, w_ref, cos_ref, sin_ref, o_ref, *, tn):
    acc = jnp.dot(a_ref[...], w_ref[...], preferred_element_type=F32)
    cos = cos_ref[...]
    sin = sin_ref[...]
    lane = lax.broadcasted_iota(jnp.int32, cos.shape, 1)
    first_half = (lane % ATT_HEAD_DIM) < (ATT_HEAD_DIM // 2)
    for j in range(tn // ATT_BLOCK):
        blk = acc[:, j * ATT_BLOCK:(j + 1) * ATT_BLOCK]
        partner = jnp.where(first_half,
                            pltpu.roll(blk, ATT_BLOCK - ATT_HEAD_DIM // 2, 1),
                            pltpu.roll(blk, ATT_HEAD_DIM // 2, 1))
        o_ref[:, j * ATT_BLOCK:(j + 1) * ATT_BLOCK] = (blk * cos + partner * sin).astype(o_ref.dtype)


def _proj_call(h, w, out_dtype, n_tok, rope=None):
    t, k = h.shape
    n = w.shape[1]
    tm = _pick(n_tok, (1152, 384, 128))
    tpb = n_tok // tm
    tn = _pick(n, (1024, 512, 256, 128))
    in_specs = [pl.BlockSpec((tm, k), lambda i, j: (i, 0)),
                pl.BlockSpec((k, tn), lambda i, j: (0, j))]
    args = [h, w]
    if rope is None:
        body = _proj_kernel
    else:
        body = functools.partial(_proj_rope_kernel, tn=tn)
        in_specs += [pl.BlockSpec((tm, ATT_BLOCK), lambda i, j: (i % tpb, 0)),
                     pl.BlockSpec((tm, ATT_BLOCK), lambda i, j: (i % tpb, 0))]
        args += list(rope)
    return pl.pallas_call(
        body,
        grid=(t // tm, n // tn),
        in_specs=in_specs,
        out_specs=pl.BlockSpec((tm, tn), lambda i, j: (i, j)),
        out_shape=jax.ShapeDtypeStruct((t, n), out_dtype),
        compiler_params=_params(("parallel", "arbitrary"), VMEM_LIMIT),
        name="in_proj_rope" if rope is not None else "in_proj",
    )(*args)


def _dt_kernel(w_ref, h_ref, b_ref, o_ref):
    raw = lax.dot_general(w_ref[...], h_ref[...], (((1,), (1,)), ((), ())), preferred_element_type=F32)
    v = raw + b_ref[...]
    o_ref[...] = jnp.maximum(v, 0.0) + jnp.log1p(jnp.exp(-jnp.abs(v)))


def _dt_call(h, w_dt_t, dt_bias, n_tok, batch):
    t, k = h.shape
    nh = w_dt_t.shape[0]
    tm = _pick(n_tok, (1152, 384, 128))
    tpb = n_tok // tm
    return pl.pallas_call(
        _dt_kernel,
        grid=(t // tm,),
        in_specs=[pl.BlockSpec((nh, k), lambda i: (0, 0)),
                  pl.BlockSpec((tm, k), lambda i: (i, 0)),
                  pl.BlockSpec((nh, 1), lambda i: (0, 0))],
        out_specs=pl.BlockSpec((None, nh, tm), lambda i: (i // tpb, 0, i % tpb)),
        out_shape=jax.ShapeDtypeStruct((batch, nh, n_tok), F32),
        compiler_params=_params(("parallel",)),
        name="dt_proj",
    )(w_dt_t, h, dt_bias)


CONV_COLS = 256
CONV_PARTS = 1


def _proj_conv_kernel(hp_ref, h_ref, hn_ref, w_ref, cw_ref, cb_ref, o_ref, win0_ref, win1_ref, res_ref,
                      *, tm, tn, tpb, n_lat):
    t = pl.program_id(0) % tpb
    tb, lb = divmod(n_lat, tm)
    halo = CONV_HALO
    no_prev = t == 0
    no_next = t == tpb - 1
    if lb == 0:
        no_prev = jnp.logical_or(no_prev, t == tb)
        no_next = jnp.logical_or(no_next, t == tb - 1)
    half = SSD_CONV // 2

    lanes = 128
    nv = (tm + 2 * halo) // 8

    wins = (win0_ref, win1_ref)
    rel = lax.broadcasted_iota(jnp.int32, (2 * halo, 1), 0) - halo

    def project(i, p):
        c0 = i * CONV_COLS
        w = w_ref[:, c0:c0 + CONV_COLS]
        r0, r1 = p * tm // CONV_PARTS, (p + 1) * tm // CONV_PARTS
        pieces = [(halo + r0, halo + r1, jnp.dot(h_ref[r0:r1, :], w, preferred_element_type=F32))]
        if p == 0:
            pieces.append((0, halo, jnp.where(no_prev, 0.0, jnp.dot(hp_ref[...], w, preferred_element_type=F32))))
        if p == CONV_PARTS - 1:
            pieces.append((halo + tm, 2 * halo + tm,
                           jnp.where(no_next, 0.0, jnp.dot(hn_ref[...], w, preferred_element_type=F32))))
        for a, b, val in pieces:
            for j in range(CONV_COLS // lanes):
                wins[i % 2][j, a:b, :] = val[:, j * lanes:(j + 1) * lanes]

    def convolve(i, p):
        win_ref = wins[i % 2]
        v0, v1 = p * nv // CONV_PARTS, (p + 1) * nv // CONV_PARTS
        for j in range(CONV_COLS // lanes):
            c = slice(i * CONV_COLS + j * lanes, i * CONV_COLS + (j + 1) * lanes)
            slabs = {u % nv: win_ref[j, pl.ds(u % nv, 8, stride=nv), :] for u in range(v0 - half, v1 + half)}
            cwb = [jnp.broadcast_to(cw_ref[k:k + 1, c], (8, lanes)) for k in range(SSD_CONV)]
            bias = jnp.broadcast_to(cb_ref[:, c], (8, lanes))
            for v in range(v0, v1):
                acc = bias
                for k in range(SSD_CONV):
                    u = v + k - half
                    if u >= nv:
                        tap = pltpu.roll(slabs[u - nv], 7, 0)
                    elif u < 0:
                        tap = pltpu.roll(slabs[u + nv], 1, 0)
                    else:
                        tap = slabs[u]
                    acc = acc + tap * cwb[k]
                res_ref[j, pl.ds(v, 8, stride=nv), :] = _silu(acc)

    def finish(i):
        win_ref = wins[i % 2]
        for j in range(CONV_COLS // lanes):
            c = slice(i * CONV_COLS + j * lanes, i * CONV_COLS + (j + 1) * lanes)
            o_ref[:, c] = res_ref[j, halo:halo + tm, :].astype(o_ref.dtype)
            if lb:
                acc = cb_ref[:, c]
                for k in range(SSD_CONV):
                    tap = win_ref[j, lb + k - half:lb + k - half + 2 * halo, :]
                    same_side = (rel < 0) == (rel + (k - half) < 0)
                    acc = acc + jnp.where(same_side, tap, 0.0) * cw_ref[k:k + 1, c]
                fixed = jnp.where(t == tb, _silu(acc), res_ref[j, lb:lb + 2 * halo, :])
                o_ref[lb - halo:lb + halo, c] = fixed.astype(o_ref.dtype)

    chunks = tn // CONV_COLS
    for p in range(CONV_PARTS):
        project(0, p)
    for i in range(chunks):
        for p in range(CONV_PARTS):
            if i + 1 < chunks:
                project(i + 1, p)
            convolve(i, p)
        finish(i)


def _proj_conv_call(h, w, conv_w, conv_b, out_dtype, n_tok, n_lat):
    t, k = h.shape
    n = w.shape[1]
    tm = _pick(n_tok, (1152, 384, 128))
    tpb = n_tok // tm
    tn = 1024
    hb = tm // CONV_HALO
    nhb = t // CONV_HALO
    return pl.pallas_call(
        functools.partial(_proj_conv_kernel, tm=tm, tn=tn, tpb=tpb, n_lat=n_lat),
        grid=(t // tm, n // tn),
        in_specs=[pl.BlockSpec((CONV_HALO, k), lambda i, j: (jnp.maximum(i * hb - 1, 0), 0)),
                  pl.BlockSpec((tm, k), lambda i, j: (i, 0)),
                  pl.BlockSpec((CONV_HALO, k), lambda i, j: (jnp.minimum((i + 1) * hb, nhb - 1), 0)),
                  pl.BlockSpec((k, tn), lambda i, j: (0, j)),
                  pl.BlockSpec((SSD_CONV, tn), lambda i, j: (0, j)),
                  pl.BlockSpec((1, tn), lambda i, j: (0, j))],
        out_specs=pl.BlockSpec((tm, tn), lambda i, j: (i, j)),
        out_shape=jax.ShapeDtypeStruct((t, n), out_dtype),
        scratch_shapes=[pltpu.VMEM((CONV_COLS // 128, tm + 2 * CONV_HALO, 128), F32)] * 3,
        compiler_params=_params(("parallel", "arbitrary"), VMEM_LIMIT),
        name="in_proj_conv",
    )(h, h, h, w, conv_w, conv_b)


ATT_GROUP = 8


def _attn_kernel(q_ref, k_ref, v_ref, lam_ref, g_ref, o_ref, *, tq, n_lat, n_tok, lam_init):
    lv = lam_ref[...]
    lam = (jnp.exp(jnp.sum(lv[0:1, :] * lv[1:2, :], axis=-1, keepdims=True))
           - jnp.exp(jnp.sum(lv[2:3, :] * lv[3:4, :], axis=-1, keepdims=True)) + lam_init)
    lane = lax.broadcasted_iota(jnp.int32, (tq, ATT_BLOCK), 1)
    nt = (((1,), (1,)), ((), ()))

    def exp_rows(s):
        e = jnp.exp2(s - jnp.max(s, axis=-1, keepdims=True))
        return e, jnp.sum(e, axis=-1, keepdims=True)

    def attend(tiles, k0, k1):
        k = k_ref[k0:k1, :]
        rows = [pl.ds(pl.multiple_of(t * tq, tq), tq) for t in tiles]
        scores = []
        for r in rows:
            q = q_ref[r, :]
            zero = jnp.zeros_like(q)
            scores.append((lax.dot_general(jnp.where(lane < ATT_HEAD_DIM, q, zero), k, nt, preferred_element_type=F32),
                           lax.dot_general(jnp.where(lane < ATT_HEAD_DIM, zero, q), k, nt, preferred_element_type=F32)))
        probs = []
        for s1, s2 in scores:
            e1, l1 = exp_rows(s1)
            e2, l2 = exp_rows(s2)
            probs.append(((e1 - (lam * l1 / l2) * e2).astype(BF16), 1.0 / l1))
        for r, (a, inv_l1) in zip(rows, probs):
            o = jnp.dot(a, v_ref[k0:k1, :], preferred_element_type=F32) * inv_l1
            o_ref[r, :] = (_rms(o, g_ref[...]) * (1.0 - lam_init)).astype(o_ref.dtype)

    lat_tiles, ctx_tiles = n_lat // tq, (n_tok - n_lat) // tq
    group = next(g for g in range(ATT_GROUP, 0, -1) if lat_tiles % g == 0)

    def latent(i, carry):
        attend([i * group + j for j in range(group)], 0, n_tok)
        return carry

    lax.fori_loop(0, lat_tiles // group, latent, 0)
    attend([lat_tiles + j for j in range(ctx_tiles)], n_lat, n_tok)


def _attn_call(qk, vzg, v_col0, lam_vecs, subln_g, lam_init, n_tok, n_lat, batch):
    t = qk.shape[0]
    tq = _pick(math.gcd(n_lat, n_tok - n_lat), (256, 128))
    vb0 = v_col0 // ATT_BLOCK
    return pl.pallas_call(
        functools.partial(_attn_kernel, tq=tq, n_lat=n_lat, n_tok=n_tok, lam_init=lam_init),
        grid=(batch, ATT_HEADS),
        in_specs=[pl.BlockSpec((n_tok, ATT_BLOCK), lambda b, h: (b, h)),
                  pl.BlockSpec((n_tok, ATT_BLOCK), lambda b, h: (b, ATT_HEADS + h)),
                  pl.BlockSpec((n_tok, ATT_BLOCK), lambda b, h: (b, vb0 + h)),
                  pl.BlockSpec((4, ATT_HEAD_DIM), lambda b, h: (0, 0)),
                  pl.BlockSpec((1, ATT_BLOCK), lambda b, h: (0, 0))],
        out_specs=pl.BlockSpec((n_tok, ATT_BLOCK), lambda b, h: (b, h)),
        out_shape=jax.ShapeDtypeStruct((t, ATT_HEADS * ATT_BLOCK), BF16),
        compiler_params=_params(("parallel", "parallel"), VMEM_LIMIT),
        name="diff_attention",
    )(qk, qk, vzg, lam_vecs, subln_g)


def _split3(v):
    hi = v.astype(BF16)
    r1 = v - hi.astype(F32)
    mid = r1.astype(BF16)
    lo = (r1 - mid.astype(F32)).astype(BF16)
    return hi, mid, lo


def _ssd_constants():
    gh, gc, n = SSD_GROUP_HEADS, SSD_GROUP_COLS, SSD_CHUNK
    j = jnp.arange(n)
    tri2 = jnp.concatenate([j[:, None] <= j[None, :], j[:, None] >= j[None, :]], axis=1)
    lane = jnp.arange(4 * gc)
    expand = jnp.arange(SSD_STAT_ROWS)[:, None] == (2 * gh + lane // SSD_HEAD_DIM)[None, :]
    return tri2.astype(BF16), expand.astype(BF16)


def _ssd_kernel(xs_ref, b_ref, c_ref, dt_ref, alog_ref, d_ref, tri2_ref, expand_ref,
                y_ref, st_ref, sst_ref, ecs_ref, dec_ref, hin_ref, *, n_lat, n_tok):
    n_chunks = n_tok // SSD_CHUNK
    lat_chunks = n_lat // SSD_CHUNK
    ctx_chunks = n_chunks - lat_chunks
    gh, gc, n = SSD_GROUP_HEADS, SSD_GROUP_COLS, SSD_CHUNK
    a8 = -jnp.exp(alog_ref[...]) * math.log2(math.e)
    row = lax.broadcasted_iota(jnp.int32, (n, n), 0)
    col = lax.broadcasted_iota(jnp.int32, (n, n), 1)
    below, above = col < row, col > row
    fwd_rows = lax.broadcasted_iota(jnp.int32, (2 * gh, n), 0) < gh
    head_of_lane = lax.broadcasted_iota(jnp.int32, (1, gc), 1) // SSD_HEAD_DIM
    pad_rows = jnp.zeros((SSD_STAT_ROWS - 6 * gh, n), F32)

    def mm3(v, m):
        return sum(jnp.dot(p, m, preferred_element_type=F32) for p in _split3(v))

    def rows(c):
        return pl.ds(pl.multiple_of(c * n, n), n)

    slot_of_lane = lax.broadcasted_iota(jnp.int32, (1, 2 * gc), 1) // SSD_HEAD_DIM

    def per_slot(v):
        out = v[2 * gh - 1:2 * gh, 0:1]
        for k in range(2 * gh - 2, -1, -1):
            out = jnp.where(slot_of_lane == k, v[k:k + 1, 0:1], out)
        return out

    group = next(k for k in (9, 6, 3, 2, 1) if n_chunks % k == 0)

    def local(i, carry):
        cs = [i * group + k for k in range(group)]
        dt8s = [dt_ref[:, rows(c)] for c in cs]
        cs2s = [mm3(dt8 * a8, tri2_ref[...]) for dt8 in dt8s]
        cs8s, colss, decs = [], [], []
        for dt8, cs2 in zip(dt8s, cs2s):
            cs8 = jnp.where(fwd_rows, cs2[:, :n], cs2[:, n:])
            tot = jnp.where(fwd_rows, cs8[:, n - 1:n], cs8[:, 0:1])
            e8 = jnp.exp2(cs8)
            w8 = dt8 * jnp.exp2(tot - cs8)
            cols = jnp.concatenate([cs8, e8, w8, pad_rows], axis=0).T
            cs8s.append(cs8)
            colss.append(cols)
            decs.append(per_slot(jnp.exp2(tot)))
        ews = [jnp.dot(cols.astype(BF16), expand_ref[...], preferred_element_type=F32) for cols in colss]
        bcs = [b_ref[rows(c), :] for c in cs]
        xcs = [xs_ref[rows(c), :] for c in cs]
        for c, ew, bc, xc, dec in zip(cs, ews, bcs, xcs, decs):
            ecs_ref[rows(c), :] = ew[:, :2 * gc]
            dec_ref[c] = jnp.broadcast_to(dec, (8, 2 * gc))
            xw = (jnp.concatenate([xc, xc], axis=1) * ew[:, 2 * gc:]).astype(BF16)
            sst_ref[c] = lax.dot_general(bc, xw, (((0,), (0,)), ((), ())), preferred_element_type=F32)
        cbs = [lax.dot_general(c_ref[rows(c), :], bc, (((1,), (1,)), ((), ())), preferred_element_type=F32)
               for c, bc in zip(cs, bcs)]
        for c, cb, cs8, cols, dt8, xc in zip(cs, cbs, cs8s, colss, dt8s, xcs):
            xb = xc.astype(BF16)
            dsum = dt8[:gh, :] + dt8[gh:, :]
            ms, xm = [], []
            for h in range(gh):
                seg = jnp.where(below, cols[:, h:h + 1] - cs8[h:h + 1, :],
                                cols[:, gh + h:gh + h + 1] - cs8[gh + h:gh + h + 1, :])
                dts = jnp.where(below, dt8[h:h + 1, :], jnp.where(above, dt8[gh + h:gh + h + 1, :], dsum[h:h + 1, :]))
                ms.append((cb * jnp.exp2(seg) * dts).astype(BF16))
                xm.append(jnp.where(head_of_lane == h, xb, jnp.zeros_like(xb)))
            y = jnp.dot(jnp.concatenate(ms, axis=1), jnp.concatenate(xm, axis=0), preferred_element_type=F32)
            y_ref[rows(c), :] = y + d_ref[...] * xc
        return carry

    lax.fori_loop(0, n_chunks // group, local, 0)

    st_ref[...] = jnp.zeros_like(st_ref)

    def scan(i, carry):
        cf = jnp.where(i < ctx_chunks, lat_chunks + i, i - ctx_chunks)
        cr = n_chunks - 1 - i
        st = st_ref[...]
        hin_ref[cf, :, :gc] = st[:, :gc].astype(BF16)
        hin_ref[cr, :, gc:] = st[:, gc:].astype(BF16)
        st_ref[:, :gc] = st[:, :gc] * dec_ref[cf][0:1, :gc] + sst_ref[cf][:, :gc]
        st_ref[:, gc:] = st[:, gc:] * dec_ref[cr][0:1, gc:] + sst_ref[cr][:, gc:]
        return carry

    lax.fori_loop(0, n_chunks, scan, 0)

    def inter(i, carry):
        cs = [i * group + k for k in range(group)]
        ts = [jnp.dot(c_ref[rows(c), :], hin_ref[c], preferred_element_type=F32) for c in cs]
        for c, t in zip(cs, ts):
            t = t * ecs_ref[rows(c), :]
            y_ref[rows(c), :] += t[:, :gc] + t[:, gc:]
        return carry

    lax.fori_loop(0, n_chunks // group, inter, 0)


def _ssd_call(xs, bcm, dt, alog, d_x, n_tok, n_lat, batch):
    t = xs.shape[0]
    gc = SSD_GROUP_COLS
    n_chunks = n_tok // SSD_CHUNK
    consts = _ssd_constants()
    const_specs = [pl.BlockSpec(m.shape, lambda b, g: (0, 0)) for m in consts]
    return pl.pallas_call(
        functools.partial(_ssd_kernel, n_lat=n_lat, n_tok=n_tok),
        grid=(batch, SSD_GROUPS),
        in_specs=[pl.BlockSpec((n_tok, gc), lambda b, g: (b, g)),
                  pl.BlockSpec((n_tok, SSD_STATE), lambda b, g: (b, g)),
                  pl.BlockSpec((n_tok, SSD_STATE), lambda b, g: (b, SSD_GROUPS + g)),
                  pl.BlockSpec((None, None, 2 * SSD_GROUP_HEADS, n_tok), lambda b, g: (b, g, 0, 0)),
                  pl.BlockSpec((None, 2 * SSD_GROUP_HEADS, SSD_CHUNK), lambda b, g: (g, 0, 0)),
                  pl.BlockSpec((1, gc), lambda b, g: (0, g))] + const_specs,
        out_specs=pl.BlockSpec((n_tok, gc), lambda b, g: (b, g)),
        out_shape=jax.ShapeDtypeStruct((t, SSD_D_INNER), F32),
        scratch_shapes=[pltpu.VMEM((SSD_STATE, 2 * gc), F32),
                        pltpu.VMEM((n_chunks, SSD_STATE, 2 * gc), F32),
                        pltpu.VMEM((n_tok, 2 * gc), F32),
                        pltpu.VMEM((n_chunks, 8, 2 * gc), F32),
                        pltpu.VMEM((n_chunks, SSD_STATE, 2 * gc), BF16)],
        compiler_params=_params(("parallel", "arbitrary"), VMEM_LIMIT),
        name="ssd_bidir",
    )(xs, bcm, bcm, dt, alog, d_x, *consts)


MIX_PARTS = 2


def _mix_out_kernel(att_ref, y_ref, z_ref, ga_ref, gs_ref, x_ref, ml_ref, mc_ref, sg_ref, ng_ref,
                    wa_ref, ws_ref, wo_ref, xo_ref, h_ref, *, tm, tpb, n_lat):
    parts = [slice(p * tm // MIX_PARTS, (p + 1) * tm // MIX_PARTS) for p in range(MIX_PARTS)]
    ssd_n = [_rms(y_ref[r, :] * _silu(z_ref[r, :].astype(F32)), sg_ref[...]).astype(BF16) for r in parts]
    acc_a = [jnp.dot(att_ref[r, :], wa_ref[...], preferred_element_type=F32) for r in parts]
    acc_s = [jnp.dot(s, ws_ref[...], preferred_element_type=F32) for s in ssd_n]
    merged = [(_sigmoid(ga_ref[r, :].astype(F32)) * a + _sigmoid(gs_ref[r, :].astype(F32)) * s).astype(BF16)
              for r, a, s in zip(parts, acc_a, acc_s)]
    outs = [jnp.dot(m, wo_ref[...], preferred_element_type=F32) for m in merged]
    for r, o in zip(parts, outs):
        is_lat = _is_latent(tm, tpb, n_lat, r)
        x_new = x_ref[r, :] + _mod(ml_ref, mc_ref, 2, is_lat) * o
        xo_ref[r, :] = x_new
        h = _rms(x_new, ng_ref[...]) * (1.0 + _mod(ml_ref, mc_ref, 4, is_lat)) + _mod(ml_ref, mc_ref, 3, is_lat)
        h_ref[r, :] = h.astype(h_ref.dtype)


def _mix_out_call(att, y, vzg, z_col0, ga_col0, gs_col0, x, mods, ssd_g, mlp_g, wa, ws, wo, n_tok, n_lat, batch):
    t, d = x.shape
    tm = _pick(n_tok, (384, 128))
    tpb = n_tok // tm
    row = lambda i: (i, 0)
    const = lambda i: (0, 0)
    return pl.pallas_call(
        functools.partial(_mix_out_kernel, tm=tm, tpb=tpb, n_lat=n_lat),
        grid=(t // tm,),
        in_specs=[pl.BlockSpec((tm, d), row),
                  pl.BlockSpec((tm, SSD_D_INNER), row),
                  pl.BlockSpec((tm, SSD_D_INNER), lambda i: (i, z_col0 // SSD_D_INNER)),
                  pl.BlockSpec((tm, d), lambda i: (i, ga_col0 // d)),
                  pl.BlockSpec((tm, d), lambda i: (i, gs_col0 // d)),
                  pl.BlockSpec((tm, d), row),
                  pl.BlockSpec((None, N_MOD, d), lambda i: (i // tpb, 0, 0)),
                  pl.BlockSpec((None, N_MOD, d), lambda i: (batch, 0, 0)),
                  pl.BlockSpec((1, SSD_D_INNER), const),
                  pl.BlockSpec((1, d), const),
                  pl.BlockSpec((d, d), const),
                  pl.BlockSpec((SSD_D_INNER, d), const),
                  pl.BlockSpec((d, d), const)],
        out_specs=[pl.BlockSpec((tm, d), row), pl.BlockSpec((tm, d), row)],
        out_shape=[jax.ShapeDtypeStruct((t, d), F32), jax.ShapeDtypeStruct((t, d), BF16)],
        compiler_params=_params(("parallel",), VMEM_LIMIT),
        name="mixer_out",
    )(att, y, vzg, vzg, vzg, x, mods, mods, ssd_g, mlp_g, wa, ws, wo)


def _mlp_kernel(*refs, tm, tpb, n_lat, emit_next):
    if emit_next:
        h_ref, x_ref, ml_ref, mc_ref, w1_ref, w2_ref, nml_ref, nmc_ref, ng_ref, o_ref, hn_ref, acc_ref = refs
    else:
        h_ref, x_ref, ml_ref, mc_ref, w1_ref, w2_ref, o_ref, acc_ref = refs
    f = pl.program_id(1)

    @pl.when(f == 0)
    def _():
        acc_ref[...] = jnp.zeros_like(acc_ref)

    u = jnp.maximum(jnp.dot(h_ref[...], w1_ref[...], preferred_element_type=F32), 0.0)
    acc_ref[...] += jnp.dot((u * u).astype(BF16), w2_ref[...], preferred_element_type=F32)

    @pl.when(f == pl.num_programs(1) - 1)
    def _():
        is_lat = _is_latent(tm, tpb, n_lat)
        x_new = x_ref[...] + _mod(ml_ref, mc_ref, 5, is_lat) * acc_ref[...]
        o_ref[...] = x_new
        if emit_next:
            hn = _rms(x_new, ng_ref[...]) * (1.0 + _mod(nml_ref, nmc_ref, 1, is_lat)) + _mod(nml_ref, nmc_ref, 0, is_lat)
            hn_ref[...] = hn.astype(hn_ref.dtype)


def _mlp_call(h, x, mods, w1, w2, n_tok, n_lat, batch, next_mods=None, next_g=None):
    t, d = x.shape
    ff = w1.shape[1]
    tm = _pick(n_tok, (768, 384, 128))
    tpb = n_tok // tm
    tf = 1024
    emit_next = next_mods is not None
    row = pl.BlockSpec((tm, d), lambda i, f: (i, 0))
    mod_l = pl.BlockSpec((None, N_MOD, d), lambda i, f: (i // tpb, 0, 0))
    mod_c = pl.BlockSpec((None, N_MOD, d), lambda i, f: (batch, 0, 0))
    in_specs = [row, row, mod_l, mod_c,
                pl.BlockSpec((d, tf), lambda i, f: (0, f)),
                pl.BlockSpec((tf, d), lambda i, f: (f, 0))]
    args = [h, x, mods, mods, w1, w2]
    out_specs, out_shape = row, jax.ShapeDtypeStruct((t, d), F32)
    if emit_next:
        in_specs += [mod_l, mod_c, pl.BlockSpec((1, d), lambda i, f: (0, 0))]
        args += [next_mods, next_mods, next_g]
        out_specs, out_shape = [row, row], [out_shape, jax.ShapeDtypeStruct((t, d), BF16)]
    return pl.pallas_call(
        functools.partial(_mlp_kernel, tm=tm, tpb=tpb, n_lat=n_lat, emit_next=emit_next),
        grid=(t // tm, ff // tf),
        in_specs=in_specs,
        out_specs=out_specs,
        out_shape=out_shape,
        scratch_shapes=[pltpu.VMEM((tm, d), F32)],
        compiler_params=_params(("parallel", "arbitrary"), VMEM_LIMIT),
        name="sq_relu_mlp",
    )(*args)


def _final_norm_kernel(x_ref, g_ref, o_ref):
    o_ref[...] = _rms(x_ref[...], g_ref[...])


def _final_norm_call(x, g, n_tok, n_lat, batch):
    d = x.shape[1]
    tr = _pick(math.gcd(n_lat, n_tok), (256, 128))
    lt = n_lat // tr
    tpb = n_tok // tr
    return pl.pallas_call(
        _final_norm_kernel,
        grid=(batch, lt),
        in_specs=[pl.BlockSpec((tr, d), lambda b, i: (b * tpb + i, 0)),
                  pl.BlockSpec((1, d), lambda b, i: (0, 0))],
        out_specs=pl.BlockSpec((None, tr, d), lambda b, i: (b, i, 0)),
        out_shape=jax.ShapeDtypeStruct((batch, n_lat, d), F32),
        compiler_params=_params(("parallel", "parallel")),
        name="final_norm",
    )(x, g)


def _rope_tables(n_lat, n_ctx):
    rows = n_lat // GRID_W
    row = jnp.broadcast_to(jnp.arange(rows)[:, None], (rows, GRID_W)).reshape(-1).astype(F32)
    col = jnp.broadcast_to(jnp.arange(GRID_W)[None, :], (rows, GRID_W)).reshape(-1).astype(F32)
    inv = jnp.float32(ROPE_BASE) ** (-jnp.arange(ROPE_PAIRS, dtype=F32) / ROPE_PAIRS)
    ang = jnp.concatenate([row[:, None] * inv, col[:, None] * inv], axis=-1)
    cos, sin = jnp.cos(ang), jnp.sin(ang)
    cos_t = jnp.concatenate([cos, cos, cos, cos], axis=-1)
    sin_t = jnp.concatenate([-sin, sin, -sin, sin], axis=-1)
    cos_t = jnp.concatenate([cos_t, jnp.ones((n_ctx, ATT_BLOCK), F32)], axis=0)
    sin_t = jnp.concatenate([sin_t, jnp.zeros((n_ctx, ATT_BLOCK), F32)], axis=0)
    return cos_t, sin_t


def kernel(x, c, ctx, c_ctx, ada_w, ada_b, norm_mix_g, w_in, conv_w, conv_b, dt_bias_f, dt_bias_b, a_log_f, a_log_b, ssd_d, ssd_norm_g, lambda_q1, lambda_k1, lambda_q2, lambda_k2, attn_subln_g, w_attn_o, w_ssd_o, w_out, norm_mlp_g, w_mlp1, w_mlp2, final_norm_g):
    batch, n_lat, d = x.shape
    n_ctx = ctx.shape[1]
    n_tok = n_lat + n_ctx
    depth = ada_w.shape[0]
    gh = SSD_GROUP_HEADS

    o_q, o_k, o_v = 0, 1024, 2048
    o_z = 3072
    o_xbc = o_z + SSD_D_INNER
    o_dt = o_xbc + SSD_D_INNER + SSD_BC
    o_ga = o_dt + 2 * SSD_HEADS
    o_gs = o_ga + D_MODEL

    scale = ATT_HEAD_DIM ** -0.5 * math.log2(math.e)
    w_qk = jnp.concatenate([w_in[:, :, o_q:o_k] * scale, w_in[:, :, o_k:o_v]], axis=-1).astype(BF16)
    w_zvg = jnp.concatenate([w_in[:, :, o_z:o_xbc], w_in[:, :, o_v:o_z], w_in[:, :, o_ga:]], axis=-1).astype(BF16)
    z_col0, v_col0, ga_col0, gs_col0 = 0, SSD_D_INNER, SSD_D_INNER + D_MODEL, SSD_D_INNER + 2 * D_MODEL
    w_xs = w_in[:, :, o_xbc:o_xbc + SSD_D_INNER].astype(BF16)
    w_bcm = w_in[:, :, o_xbc + SSD_D_INNER:o_dt].astype(BF16)
    perm = jnp.asarray([f * SSD_HEADS + g * gh + h for g in range(SSD_GROUPS) for f in range(2) for h in range(gh)])
    w_dt_t = jnp.swapaxes(w_in[:, :, o_dt:o_ga][:, :, perm], 1, 2).astype(BF16)
    dt_bias = jnp.concatenate([dt_bias_f, dt_bias_b], axis=-1)[:, perm][:, :, None]
    a_log = jnp.concatenate([a_log_f, a_log_b], axis=-1)[:, perm].reshape(depth, SSD_GROUPS, 2 * gh, 1)
    a_log = jnp.broadcast_to(a_log, (depth, SSD_GROUPS, 2 * gh, SSD_CHUNK)).astype(F32)
    d_x = jnp.repeat(ssd_d, SSD_HEAD_DIM, axis=-1)[:, None, :].astype(F32)
    wa_b, ws_b, wo_b = w_attn_o.astype(BF16), w_ssd_o.astype(BF16), w_out.astype(BF16)
    w1_b, w2_b = w_mlp1.astype(BF16), w_mlp2.astype(BF16)
    lam_vecs = jnp.stack([lambda_q1, lambda_k1, lambda_q2, lambda_k2], axis=1).astype(F32)
    rope = _rope_tables(n_lat, n_ctx)

    rows_pad = -(-(batch + 1) // MOD_ROWS_PAD) * MOD_ROWS_PAD
    c_all = jnp.concatenate([c, c_ctx[None, :], jnp.zeros((rows_pad - batch - 1, d), F32)], axis=0)
    mods = _mod_call(c_all, ada_w, ada_b).reshape(depth, rows_pad, N_MOD, d)

    xt, h = _embed_call(x, ctx, mods[0], norm_mix_g[0][None, :])
    for l in range(depth):
        lam_init = 0.8 - 0.6 * math.exp(-0.3 * l)
        qk = _proj_call(h, w_qk[l], BF16, n_tok, rope=rope)
        zvg = _proj_call(h, w_zvg[l], BF16, n_tok)
        dt = _dt_call(h, w_dt_t[l], dt_bias[l], n_tok, batch)
        dt = dt.reshape(batch, SSD_GROUPS, 2 * gh, n_tok)
        xs = _proj_conv_call(h, w_xs[l], conv_w[l, :, :SSD_D_INNER], conv_b[l, None, :SSD_D_INNER], F32, n_tok, n_lat)
        bcm = _proj_conv_call(h, w_bcm[l], conv_w[l, :, SSD_D_INNER:], conv_b[l, None, SSD_D_INNER:], BF16, n_tok, n_lat)
        att = _attn_call(qk, zvg, v_col0, lam_vecs[l], attn_subln_g[l][None, :], lam_init, n_tok, n_lat, batch)
        y = _ssd_call(xs, bcm, dt, a_log[l], d_x[l], n_tok, n_lat, batch)
        xt, h2 = _mix_out_call(att, y, zvg, z_col0, ga_col0, gs_col0, xt, mods[l], ssd_norm_g[l][None, :],
                               norm_mlp_g[l][None, :], wa_b[l], ws_b[l], wo_b[l], n_tok, n_lat, batch)
        if l + 1 < depth:
            xt, h = _mlp_call(h2, xt, mods[l], w1_b[l], w2_b[l], n_tok, n_lat, batch,
                              next_mods=mods[l + 1], next_g=norm_mix_g[l + 1][None, :])
        else:
            xt = _mlp_call(h2, xt, mods[l], w1_b[l], w2_b[l], n_tok, n_lat, batch)
    return _final_norm_call(xt, final_norm_g[None, :], n_tok, n_lat, batch)
```

```python
import functools
import math

import jax
import jax.numpy as jnp
from jax import lax
from jax.experimental import pallas as pl
from jax.experimental.pallas import tpu as pltpu

F32 = jnp.float32
BF16 = jnp.bfloat16

D_MODEL = 1024
GRID_W = 64
ATT_HEADS = 8
ATT_HEAD_DIM = 64
ATT_BLOCK = 2 * ATT_HEAD_DIM
ROPE_BASE = 10000.0
ROPE_PAIRS = ATT_HEAD_DIM // 4
SSD_D_INNER = 2 * D_MODEL
SSD_HEAD_DIM = 64
SSD_HEADS = SSD_D_INNER // SSD_HEAD_DIM
SSD_GROUPS = 8
SSD_GROUP_HEADS = SSD_HEADS // SSD_GROUPS
SSD_GROUP_COLS = SSD_GROUP_HEADS * SSD_HEAD_DIM
SSD_STATE = 128
SSD_CONV = 5
SSD_CHUNK = 128
SSD_BC = 2 * SSD_GROUPS * SSD_STATE
SSD_STAT_ROWS = 32
D_FF = 4 * D_MODEL
N_MOD = 6
EPS = 1e-6
CONV_HALO = 16
MOD_ROWS_PAD = 8
VMEM_LIMIT = 56 * 1024 * 1024


def _params(sem, vmem=None):
    return pltpu.CompilerParams(dimension_semantics=sem, vmem_limit_bytes=vmem)


def _pick(n, candidates):
    for c in candidates:
        if n % c == 0:
            return c
    raise ValueError(f"no tile in {candidates} divides {n}")


def _silu(v):
    h = 0.5 * v
    return h + h * jnp.tanh(h)


def _sigmoid(v):
    return 0.5 * jnp.tanh(0.5 * v) + 0.5


def _rms(v, g):
    return v * lax.rsqrt(jnp.mean(v * v, axis=-1, keepdims=True) + EPS) * g


def _is_latent(tm, tiles_per_batch, n_lat, part=slice(None)):
    off = (pl.program_id(0) % tiles_per_batch) * tm
    r0, r1, _ = part.indices(tm)
    rows = lax.broadcasted_iota(jnp.int32, (r1 - r0, 1), 0) + r0
    return rows < (n_lat - off)


def _mod(ml_ref, mc_ref, k, is_lat):
    return jnp.where(is_lat, ml_ref[k:k + 1, :], mc_ref[k:k + 1, :])


def _mod_kernel(c_ref, w_ref, b_ref, o_ref):
    a = _silu(c_ref[...]).astype(BF16)
    o_ref[...] = jnp.dot(a, w_ref[...].astype(BF16), preferred_element_type=F32) + b_ref[...]


def _mod_call(c_all, ada_w, ada_b):
    depth, d, nm = ada_w.shape
    rows = c_all.shape[0]
    tn = 1024
    return pl.pallas_call(
        _mod_kernel,
        grid=(depth, nm // tn),
        in_specs=[pl.BlockSpec((rows, d), lambda l, j: (0, 0)),
                  pl.BlockSpec((None, d, tn), lambda l, j: (l, 0, j)),
                  pl.BlockSpec((None, 1, tn), lambda l, j: (l, 0, j))],
        out_specs=pl.BlockSpec((None, rows, tn), lambda l, j: (l, 0, j)),
        out_shape=jax.ShapeDtypeStruct((depth, rows, nm), F32),
        compiler_params=_params(("arbitrary", "arbitrary")),
        name="adaln_mod",
    )(c_all, ada_w, ada_b.reshape(depth, 1, nm))


def _embed_kernel(x_ref, ctx_ref, ml_ref, mc_ref, g_ref, xt_ref, h_ref, *, lat_tiles):
    is_lat = pl.program_id(1) < lat_tiles
    x = jnp.where(is_lat, x_ref[...], ctx_ref[...])
    m = jnp.where(is_lat, ml_ref[...], mc_ref[...])
    xt_ref[...] = x
    h_ref[...] = (_rms(x, g_ref[...]) * (1.0 + m[1:2, :]) + m[0:1, :]).astype(h_ref.dtype)


def _embed_call(x, ctx, mods, g):
    batch, n_lat, d = x.shape
    n_ctx = ctx.shape[1]
    n_tok = n_lat + n_ctx
    tr = _pick(math.gcd(n_lat, n_ctx), (256, 128))
    lt, tpb = n_lat // tr, n_tok // tr
    row = pl.BlockSpec((tr, d), lambda b, i: (b * tpb + i, 0))
    return pl.pallas_call(
        functools.partial(_embed_kernel, lat_tiles=lt),
        grid=(batch, tpb),
        in_specs=[pl.BlockSpec((None, tr, d), lambda b, i: (b, jnp.minimum(i, lt - 1), 0)),
                  pl.BlockSpec((None, tr, d), lambda b, i: (b, jnp.maximum(i - lt, 0), 0)),
                  pl.BlockSpec((None, N_MOD, d), lambda b, i: (b, 0, 0)),
                  pl.BlockSpec((None, N_MOD, d), lambda b, i: (batch, 0, 0)),
                  pl.BlockSpec((1, d), lambda b, i: (0, 0))],
        out_specs=[row, row],
        out_shape=[jax.ShapeDtypeStruct((batch * n_tok, d), F32), jax.ShapeDtypeStruct((batch * n_tok, d), BF16)],
        compiler_params=_params(("parallel", "arbitrary")),
        name="embed_norm_modulate",
    )(x, ctx, mods, mods, g)


def _proj_kernel(a_ref, w_ref, o_ref):
    o_ref[...] = jnp.dot(a_ref[...], w_ref[...], preferred_element_type=F32).astype(o_ref.dtype)


def _proj_rope_kernel(a_ref, w_ref, cos_ref, sin_ref, o_ref, *, tn):
    acc = jnp.dot(a_ref[...], w_ref[...], preferred_element_type=F32)
    cos = cos_ref[...]
    sin = sin_ref[...]
    lane = lax.broadcasted_iota(jnp.int32, cos.shape, 1)
    first_half = (lane % ATT_HEAD_DIM) < (ATT_HEAD_DIM // 2)
    for j in range(tn // ATT_BLOCK):
        blk = acc[:, j * ATT_BLOCK:(j + 1) * ATT_BLOCK]
        partner = jnp.where(first_half,
                            pltpu.roll(blk, ATT_BLOCK - ATT_HEAD_DIM // 2, 1),
                            pltpu.roll(blk, ATT_HEAD_DIM // 2, 1))
        o_ref[:, j * ATT_BLOCK:(j + 1) * ATT_BLOCK] = (blk * cos + partner * sin).astype(o_ref.dtype)


def _proj_call(h, w, out_dtype, n_tok, rope=None):
    t, k = h.shape
    n = w.shape[1]
    tm = _pick(n_tok, (1152, 384, 128))
    tpb = n_tok // tm
    tn = _pick(n, (1024, 512, 256, 128))
    in_specs = [pl.BlockSpec((tm, k), lambda i, j: (i, 0)),
                pl.BlockSpec((k, tn), lambda i, j: (0, j))]
    args = [h, w]
    if rope is None:
        body = _proj_kernel
    else:
        body = functools.partial(_proj_rope_kernel, tn=tn)
        in_specs += [pl.BlockSpec((tm, ATT_BLOCK), lambda i, j: (i % tpb, 0)),
                     pl.BlockSpec((tm, ATT_BLOCK), lambda i, j: (i % tpb, 0))]
        args += list(rope)
    return pl.pallas_call(
        body,
        grid=(t // tm, n // tn),
        in_specs=in_specs,
        out_specs=pl.BlockSpec((tm, tn), lambda i, j: (i, j)),
        out_shape=jax.ShapeDtypeStruct((t, n), out_dtype),
        compiler_params=_params(("parallel", "arbitrary"), VMEM_LIMIT),
        name="in_proj_rope" if rope is not None else "in_proj",
    )(*args)


def _dt_kernel(w_ref, h_ref, b_ref, o_ref):
    raw = lax.dot_general(w_ref[...], h_ref[...], (((1,), (1,)), ((), ())), preferred_element_type=F32)
    v = raw + b_ref[...]
    o_ref[...] = jnp.maximum(v, 0.0) + jnp.log1p(jnp.exp(-jnp.abs(v)))


def _dt_call(h, w_dt_t, dt_bias, n_tok, batch):
    t, k = h.shape
    nh = w_dt_t.shape[0]
    tm = _pick(n_tok, (1152, 384, 128))
    tpb = n_tok // tm
    return pl.pallas_call(
        _dt_kernel,
        grid=(t // tm,),
        in_specs=[pl.BlockSpec((nh, k), lambda i: (0, 0)),
                  pl.BlockSpec((tm, k), lambda i: (i, 0)),
                  pl.BlockSpec((nh, 1), lambda i: (0, 0))],
        out_specs=pl.BlockSpec((None, nh, tm), lambda i: (i // tpb, 0, i % tpb)),
        out_shape=jax.ShapeDtypeStruct((batch, nh, n_tok), F32),
        compiler_params=_params(("parallel",)),
        name="dt_proj",
    )(w_dt_t, h, dt_bias)


CONV_COLS = 256
CONV_PARTS = 1


def _proj_conv_kernel(hp_ref, h_ref, hn_ref, w_ref, cw_ref, cb_ref, o_ref, win0_ref, win1_ref, res_ref,
                      *, tm, tn, tpb, n_lat):
    t = pl.program_id(0) % tpb
    tb, lb = divmod(n_lat, tm)
    halo = CONV_HALO
    no_prev = t == 0
    no_next = t == tpb - 1
    if lb == 0:
        no_prev = jnp.logical_or(no_prev, t == tb)
        no_next = jnp.logical_or(no_next, t == tb - 1)
    half = SSD_CONV // 2

    lanes = 128
    nv = (tm + 2 * halo) // 8

    wins = (win0_ref, win1_ref)
    rel = lax.broadcasted_iota(jnp.int32, (2 * halo, 1), 0) - halo

    def project(i, p):
        c0 = i * CONV_COLS
        w = w_ref[:, c0:c0 + CONV_COLS]
        r0, r1 = p * tm // CONV_PARTS, (p + 1) * tm // CONV_PARTS
        pieces = [(halo + r0, halo + r1, jnp.dot(h_ref[r0:r1, :], w, preferred_element_type=F32))]
        if p == 0:
            pieces.append((0, halo, jnp.where(no_prev, 0.0, jnp.dot(hp_ref[...], w, preferred_element_type=F32))))
        if p == CONV_PARTS - 1:
            pieces.append((halo + tm, 2 * halo + tm,
                           jnp.where(no_next, 0.0, jnp.dot(hn_ref[...], w, preferred_element_type=F32))))
        for a, b, val in pieces:
            for j in range(CONV_COLS // lanes):
                wins[i % 2][j, a:b, :] = val[:, j * lanes:(j + 1) * lanes]

    def convolve(i, p):
        win_ref = wins[i % 2]
        v0, v1 = p * nv // CONV_PARTS, (p + 1) * nv // CONV_PARTS
        for j in range(CONV_COLS // lanes):
            c = slice(i * CONV_COLS + j * lanes, i * CONV_COLS + (j + 1) * lanes)
            slabs = {u % nv: win_ref[j, pl.ds(u % nv, 8, stride=nv), :] for u in range(v0 - half, v1 + half)}
            cwb = [jnp.broadcast_to(cw_ref[k:k + 1, c], (8, lanes)) for k in range(SSD_CONV)]
            bias = jnp.broadcast_to(cb_ref[:, c], (8, lanes))
            for v in range(v0, v1):
                acc = bias
                for k in range(SSD_CONV):
                    u = v + k - half
                    if u >= nv:
                        tap = pltpu.roll(slabs[u - nv], 7, 0)
                    elif u < 0:
                        tap = pltpu.roll(slabs[u + nv], 1, 0)
                    else:
                        tap = slabs[u]
                    acc = acc + tap * cwb[k]
                res_ref[j, pl.ds(v, 8, stride=nv), :] = _silu(acc)

    def finish(i):
        win_ref = wins[i % 2]
        for j in range(CONV_COLS // lanes):
            c = slice(i * CONV_COLS + j * lanes, i * CONV_COLS + (j + 1) * lanes)
            o_ref[:, c] = res_ref[j, halo:halo + tm, :].astype(o_ref.dtype)
            if lb:
                acc = cb_ref[:, c]
                for k in range(SSD_CONV):
                    tap = win_ref[j, lb + k - half:lb + k - half + 2 * halo, :]
                    same_side = (rel < 0) == (rel + (k - half) < 0)
                    acc = acc + jnp.where(same_side, tap, 0.0) * cw_ref[k:k + 1, c]
                fixed = jnp.where(t == tb, _silu(acc), res_ref[j, lb:lb + 2 * halo, :])
                o_ref[lb - halo:lb + halo, c] = fixed.astype(o_ref.dtype)

    chunks = tn // CONV_COLS
    for p in range(CONV_PARTS):
        project(0, p)
    for i in range(chunks):
        for p in range(CONV_PARTS):
            if i + 1 < chunks:
                project(i + 1, p)
            convolve(i, p)
        finish(i)


def _proj_conv_call(h, w, conv_w, conv_b, out_dtype, n_tok, n_lat):
    t, k = h.shape
    n = w.shape[1]
    tm = _pick(n_tok, (1152, 384, 128))
    tpb = n_tok // tm
    tn = 1024
    hb = tm // CONV_HALO
    nhb = t // CONV_HALO
    return pl.pallas_call(
        functools.partial(_proj_conv_kernel, tm=tm, tn=tn, tpb=tpb, n_lat=n_lat),
        grid=(t // tm, n // tn),
        in_specs=[pl.BlockSpec((CONV_HALO, k), lambda i, j: (jnp.maximum(i * hb - 1, 0), 0)),
                  pl.BlockSpec((tm, k), lambda i, j: (i, 0)),
                  pl.BlockSpec((CONV_HALO, k), lambda i, j: (jnp.minimum((i + 1) * hb, nhb - 1), 0)),
                  pl.BlockSpec((k, tn), lambda i, j: (0, j)),
                  pl.BlockSpec((SSD_CONV, tn), lambda i, j: (0, j)),
                  pl.BlockSpec((1, tn), lambda i, j: (0, j))],
        out_specs=pl.BlockSpec((tm, tn), lambda i, j: (i, j)),
        out_shape=jax.ShapeDtypeStruct((t, n), out_dtype),
        scratch_shapes=[pltpu.VMEM((CONV_COLS // 128, tm + 2 * CONV_HALO, 128), F32)] * 3,
        compiler_params=_params(("parallel", "arbitrary"), VMEM_LIMIT),
        name="in_proj_conv",
    )(h, h, h, w, conv_w, conv_b)


ATT_GROUP = 8


def _attn_kernel(q_ref, k_ref, v_ref, lam_ref, g_ref, o_ref, *, tq, n_lat, n_tok, lam_init):
    lv = lam_ref[...]
    lam = (jnp.exp(jnp.sum(lv[0:1, :] * lv[1:2, :], axis=-1, keepdims=True))
           - jnp.exp(jnp.sum(lv[2:3, :] * lv[3:4, :], axis=-1, keepdims=True)) + lam_init)
    lane = lax.broadcasted_iota(jnp.int32, (tq, ATT_BLOCK), 1)
    nt = (((1,), (1,)), ((), ()))

    def exp_rows(s):
        e = jnp.exp2(s - jnp.max(s, axis=-1, keepdims=True))
        return e, jnp.sum(e, axis=-1, keepdims=True)

    def attend(tiles, k0, k1):
        k = k_ref[k0:k1, :]
        rows = [pl.ds(pl.multiple_of(t * tq, tq), tq) for t in tiles]
        scores = []
        for r in rows:
            q = q_ref[r, :]
            zero = jnp.zeros_like(q)
            scores.append((lax.dot_general(jnp.where(lane < ATT_HEAD_DIM, q, zero), k, nt, preferred_element_type=F32),
                           lax.dot_general(jnp.where(lane < ATT_HEAD_DIM, zero, q), k, nt, preferred_element_type=F32)))
        probs = []
        for s1, s2 in scores:
            e1, l1 = exp_rows(s1)
            e2, l2 = exp_rows(s2)
            probs.append(((e1 - (lam * l1 / l2) * e2).astype(BF16), 1.0 / l1))
        for r, (a, inv_l1) in zip(rows, probs):
            o = jnp.dot(a, v_ref[k0:k1, :], preferred_element_type=F32) * inv_l1
            o_ref[r, :] = (_rms(o, g_ref[...]) * (1.0 - lam_init)).astype(o_ref.dtype)

    lat_tiles, ctx_tiles = n_lat // tq, (n_tok - n_lat) // tq
    group = next(g for g in range(ATT_GROUP, 0, -1) if lat_tiles % g == 0)

    def latent(i, carry):
        attend([i * group + j for j in range(group)], 0, n_tok)
        return carry

    lax.fori_loop(0, lat_tiles // group, latent, 0)
    attend([lat_tiles + j for j in range(ctx_tiles)], n_lat, n_tok)


def _attn_call(qk, vzg, v_col0, lam_vecs, subln_g, lam_init, n_tok, n_lat, batch):
    t = qk.shape[0]
    tq = _pick(math.gcd(n_lat, n_tok - n_lat), (256, 128))
    vb0 = v_col0 // ATT_BLOCK
    return pl.pallas_call(
        functools.partial(_attn_kernel, tq=tq, n_lat=n_lat, n_tok=n_tok, lam_init=lam_init),
        grid=(batch, ATT_HEADS),
        in_specs=[pl.BlockSpec((n_tok, ATT_BLOCK), lambda b, h: (b, h)),
                  pl.BlockSpec((n_tok, ATT_BLOCK), lambda b, h: (b, ATT_HEADS + h)),
                  pl.BlockSpec((n_tok, ATT_BLOCK), lambda b, h: (b, vb0 + h)),
                  pl.BlockSpec((4, ATT_HEAD_DIM), lambda b, h: (0, 0)),
                  pl.BlockSpec((1, ATT_BLOCK), lambda b, h: (0, 0))],
        out_specs=pl.BlockSpec((n_tok, ATT_BLOCK), lambda b, h: (b, h)),
        out_shape=jax.ShapeDtypeStruct((t, ATT_HEADS * ATT_BLOCK), BF16),
        compiler_params=_params(("parallel", "parallel"), VMEM_LIMIT),
        name="diff_attention",
    )(qk, qk, vzg, lam_vecs, subln_g)


def _split3(v):
    hi = v.astype(BF16)
    r1 = v - hi.astype(F32)
    mid = r1.astype(BF16)
    lo = (r1 - mid.astype(F32)).astype(BF16)
    return hi, mid, lo


def _ssd_constants():
    gh, gc, n = SSD_GROUP_HEADS, SSD_GROUP_COLS, SSD_CHUNK
    j = jnp.arange(n)
    tri2 = jnp.concatenate([j[:, None] <= j[None, :], j[:, None] >= j[None, :]], axis=1)
    lane = jnp.arange(4 * gc)
    expand = jnp.arange(SSD_STAT_ROWS)[:, None] == (2 * gh + lane // SSD_HEAD_DIM)[None, :]
    return tri2.astype(BF16), expand.astype(BF16)


def _ssd_kernel(xs_ref, b_ref, c_ref, dt_ref, alog_ref, d_ref, tri2_ref, expand_ref,
                y_ref, st_ref, sst_ref, ecs_ref, dec_ref, hin_ref, *, n_lat, n_tok):
    n_chunks = n_tok // SSD_CHUNK
    lat_chunks = n_lat // SSD_CHUNK
    ctx_chunks = n_chunks - lat_chunks
    gh, gc, n = SSD_GROUP_HEADS, SSD_GROUP_COLS, SSD_CHUNK
    a8 = -jnp.exp(alog_ref[...]) * math.log2(math.e)
    row = lax.broadcasted_iota(jnp.int32, (n, n), 0)
    col = lax.broadcasted_iota(jnp.int32, (n, n), 1)
    below, above = col < row, col > row
    fwd_rows = lax.broadcasted_iota(jnp.int32, (2 * gh, n), 0) < gh
    head_of_lane = lax.broadcasted_iota(jnp.int32, (1, gc), 1) // SSD_HEAD_DIM
    pad_rows = jnp.zeros((SSD_STAT_ROWS - 6 * gh, n), F32)

    def mm3(v, m):
        return sum(jnp.dot(p, m, preferred_element_type=F32) for p in _split3(v))

    def rows(c):
        return pl.ds(pl.multiple_of(c * n, n), n)

    slot_of_lane = lax.broadcasted_iota(jnp.int32, (1, 2 * gc), 1) // SSD_HEAD_DIM

    def per_slot(v):
        out = v[2 * gh - 1:2 * gh, 0:1]
        for k in range(2 * gh - 2, -1, -1):
            out = jnp.where(slot_of_lane == k, v[k:k + 1, 0:1], out)
        return out

    group = next(k for k in (18, 9, 6, 3, 2, 1) if n_chunks % k == 0)

    def local(i, carry):
        cs = [i * group + k for k in range(group)]
        dt8s = [dt_ref[:, rows(c)] for c in cs]
        cs2s = [mm3(dt8 * a8, tri2_ref[...]) for dt8 in dt8s]
        cs8s, colss, decs = [], [], []
        for dt8, cs2 in zip(dt8s, cs2s):
            cs8 = jnp.where(fwd_rows, cs2[:, :n], cs2[:, n:])
            tot = jnp.where(fwd_rows, cs8[:, n - 1:n], cs8[:, 0:1])
            e8 = jnp.exp2(cs8)
            w8 = dt8 * jnp.exp2(tot - cs8)
            cols = jnp.concatenate([cs8, e8, w8, pad_rows], axis=0).T
            cs8s.append(cs8)
            colss.append(cols)
            decs.append(per_slot(jnp.exp2(tot)))
        ews = [jnp.dot(cols.astype(BF16), expand_ref[...], preferred_element_type=F32) for cols in colss]
        bcs = [b_ref[rows(c), :] for c in cs]
        xcs = [xs_ref[rows(c), :] for c in cs]
        for c, ew, bc, xc, dec in zip(cs, ews, bcs, xcs, decs):
            ecs_ref[rows(c), :] = ew[:, :2 * gc]
            dec_ref[c] = jnp.broadcast_to(dec, (8, 2 * gc))
            xw = (jnp.concatenate([xc, xc], axis=1) * ew[:, 2 * gc:]).astype(BF16)
            sst_ref[c] = lax.dot_general(bc, xw, (((0,), (0,)), ((), ())), preferred_element_type=F32)
        cbs = [lax.dot_general(c_ref[rows(c), :], bc, (((1,), (1,)), ((), ())), preferred_element_type=F32)
               for c, bc in zip(cs, bcs)]
        for c, cb, cs8, cols, dt8, xc in zip(cs, cbs, cs8s, colss, dt8s, xcs):
            xb = xc.astype(BF16)
            dsum = dt8[:gh, :] + dt8[gh:, :]
            ms, xm = [], []
            for h in range(gh):
                seg = jnp.where(below, cols[:, h:h + 1] - cs8[h:h + 1, :],
                                cols[:, gh + h:gh + h + 1] - cs8[gh + h:gh + h + 1, :])
                dts = jnp.where(below, dt8[h:h + 1, :], jnp.where(above, dt8[gh + h:gh + h + 1, :], dsum[h:h + 1, :]))
                ms.append((cb * jnp.exp2(seg) * dts).astype(BF16))
                xm.append(jnp.where(head_of_lane == h, xb, jnp.zeros_like(xb)))
            y = jnp.dot(jnp.concatenate(ms, axis=1), jnp.concatenate(xm, axis=0), preferred_element_type=F32)
            y_ref[rows(c), :] = y + d_ref[...] * xc
        return carry

    lax.fori_loop(0, n_chunks // group, local, 0)

    st_ref[...] = jnp.zeros_like(st_ref)

    def scan(i, carry):
        cf = jnp.where(i < ctx_chunks, lat_chunks + i, i - ctx_chunks)
        cr = n_chunks - 1 - i
        st = st_ref[...]
        hin_ref[cf, :, :gc] = st[:, :gc].astype(BF16)
        hin_ref[cr, :, gc:] = st[:, gc:].astype(BF16)
        st_ref[:, :gc] = st[:, :gc] * dec_ref[cf][0:1, :gc] + sst_ref[cf][:, :gc]
        st_ref[:, gc:] = st[:, gc:] * dec_ref[cr][0:1, gc:] + sst_ref[cr][:, gc:]
        return carry

    lax.fori_loop(0, n_chunks, scan, 0)

    def inter(i, carry):
        cs = [i * group + k for k in range(group)]
        ts = [jnp.dot(c_ref[rows(c), :], hin_ref[c], preferred_element_type=F32) for c in cs]
        for c, t in zip(cs, ts):
            t = t * ecs_ref[rows(c), :]
            y_ref[rows(c), :] += t[:, :gc] + t[:, gc:]
        return carry

    lax.fori_loop(0, n_chunks // group, inter, 0)


def _ssd_call(xs, bcm, dt, alog, d_x, n_tok, n_lat, batch):
    t = xs.shape[0]
    gc = SSD_GROUP_COLS
    n_chunks = n_tok // SSD_CHUNK
    consts = _ssd_constants()
    const_specs = [pl.BlockSpec(m.shape, lambda b, g: (0, 0)) for m in consts]
    return pl.pallas_call(
        functools.partial(_ssd_kernel, n_lat=n_lat, n_tok=n_tok),
        grid=(batch, SSD_GROUPS),
        in_specs=[pl.BlockSpec((n_tok, gc), lambda b, g: (b, g)),
                  pl.BlockSpec((n_tok, SSD_STATE), lambda b, g: (b, g)),
                  pl.BlockSpec((n_tok, SSD_STATE), lambda b, g: (b, SSD_GROUPS + g)),
                  pl.BlockSpec((None, None, 2 * SSD_GROUP_HEADS, n_tok), lambda b, g: (b, g, 0, 0)),
                  pl.BlockSpec((None, 2 * SSD_GROUP_HEADS, SSD_CHUNK), lambda b, g: (g, 0, 0)),
                  pl.BlockSpec((1, gc), lambda b, g: (0, g))] + const_specs,
        out_specs=pl.BlockSpec((n_tok, gc), lambda b, g: (b, g)),
        out_shape=jax.ShapeDtypeStruct((t, SSD_D_INNER), F32),
        scratch_shapes=[pltpu.VMEM((SSD_STATE, 2 * gc), F32),
                        pltpu.VMEM((n_chunks, SSD_STATE, 2 * gc), F32),
                        pltpu.VMEM((n_tok, 2 * gc), F32),
                        pltpu.VMEM((n_chunks, 8, 2 * gc), F32),
                        pltpu.VMEM((n_chunks, SSD_STATE, 2 * gc), BF16)],
        compiler_params=_params(("parallel", "arbitrary"), VMEM_LIMIT),
        name="ssd_bidir",
    )(xs, bcm, bcm, dt, alog, d_x, *consts)


MIX_PARTS = 2


def _mix_out_kernel(att_ref, y_ref, z_ref, ga_ref, gs_ref, x_ref, ml_ref, mc_ref, sg_ref, ng_ref,
                    wa_ref, ws_ref, wo_ref, xo_ref, h_ref, *, tm, tpb, n_lat):
    parts = [slice(p * tm // MIX_PARTS, (p + 1) * tm // MIX_PARTS) for p in range(MIX_PARTS)]
    ssd_n = [_rms(y_ref[r, :] * _silu(z_ref[r, :].astype(F32)), sg_ref[...]).astype(BF16) for r in parts]
    acc_a = [jnp.dot(att_ref[r, :], wa_ref[...], preferred_element_type=F32) for r in parts]
    acc_s = [jnp.dot(s, ws_ref[...], preferred_element_type=F32) for s in ssd_n]
    merged = [(_sigmoid(ga_ref[r, :].astype(F32)) * a + _sigmoid(gs_ref[r, :].astype(F32)) * s).astype(BF16)
              for r, a, s in zip(parts, acc_a, acc_s)]
    outs = [jnp.dot(m, wo_ref[...], preferred_element_type=F32) for m in merged]
    for r, o in zip(parts, outs):
        is_lat = _is_latent(tm, tpb, n_lat, r)
        x_new = x_ref[r, :] + _mod(ml_ref, mc_ref, 2, is_lat) * o
        xo_ref[r, :] = x_new
        h = _rms(x_new, ng_ref[...]) * (1.0 + _mod(ml_ref, mc_ref, 4, is_lat)) + _mod(ml_ref, mc_ref, 3, is_lat)
        h_ref[r, :] = h.astype(h_ref.dtype)


def _mix_out_call(att, y, vzg, z_col0, ga_col0, gs_col0, x, mods, ssd_g, mlp_g, wa, ws, wo, n_tok, n_lat, batch):
    t, d = x.shape
    tm = _pick(n_tok, (384, 128))
    tpb = n_tok // tm
    row = lambda i: (i, 0)
    const = lambda i: (0, 0)
    return pl.pallas_call(
        functools.partial(_mix_out_kernel, tm=tm, tpb=tpb, n_lat=n_lat),
        grid=(t // tm,),
        in_specs=[pl.BlockSpec((tm, d), row),
                  pl.BlockSpec((tm, SSD_D_INNER), row),
                  pl.BlockSpec((tm, SSD_D_INNER), lambda i: (i, z_col0 // SSD_D_INNER)),
                  pl.BlockSpec((tm, d), lambda i: (i, ga_col0 // d)),
                  pl.BlockSpec((tm, d), lambda i: (i, gs_col0 // d)),
                  pl.BlockSpec((tm, d), row),
                  pl.BlockSpec((None, N_MOD, d), lambda i: (i // tpb, 0, 0)),
                  pl.BlockSpec((None, N_MOD, d), lambda i: (batch, 0, 0)),
                  pl.BlockSpec((1, SSD_D_INNER), const),
                  pl.BlockSpec((1, d), const),
                  pl.BlockSpec((d, d), const),
                  pl.BlockSpec((SSD_D_INNER, d), const),
                  pl.BlockSpec((d, d), const)],
        out_specs=[pl.BlockSpec((tm, d), row), pl.BlockSpec((tm, d), row)],
        out_shape=[jax.ShapeDtypeStruct((t, d), F32), jax.ShapeDtypeStruct((t, d), BF16)],
        compiler_params=_params(("parallel",), VMEM_LIMIT),
        name="mixer_out",
    )(att, y, vzg, vzg, vzg, x, mods, mods, ssd_g, mlp_g, wa, ws, wo)


def _mlp_kernel(*refs, tm, tpb, n_lat, emit_next):
    if emit_next:
        h_ref, x_ref, ml_ref, mc_ref, w1_ref, w2_ref, nml_ref, nmc_ref, ng_ref, o_ref, hn_ref, acc_ref = refs
    else:
        h_ref, x_ref, ml_ref, mc_ref, w1_ref, w2_ref, o_ref, acc_ref = refs
    f = pl.program_id(1)

    @pl.when(f == 0)
    def _():
        acc_ref[...] = jnp.zeros_like(acc_ref)

    u = jnp.maximum(jnp.dot(h_ref[...], w1_ref[...], preferred_element_type=F32), 0.0)
    acc_ref[...] += jnp.dot((u * u).astype(BF16), w2_ref[...], preferred_element_type=F32)

    @pl.when(f == pl.num_programs(1) - 1)
    def _():
        is_lat = _is_latent(tm, tpb, n_lat)
        x_new = x_ref[...] + _mod(ml_ref, mc_ref, 5, is_lat) * acc_ref[...]
        o_ref[...] = x_new
        if emit_next:
            hn = _rms(x_new, ng_ref[...]) * (1.0 + _mod(nml_ref, nmc_ref, 1, is_lat)) + _mod(nml_ref, nmc_ref, 0, is_lat)
            hn_ref[...] = hn.astype(hn_ref.dtype)


def _mlp_call(h, x, mods, w1, w2, n_tok, n_lat, batch, next_mods=None, next_g=None):
    t, d = x.shape
    ff = w1.shape[1]
    tm = _pick(n_tok, (768, 384, 128))
    tpb = n_tok // tm
    tf = 1024
    emit_next = next_mods is not None
    row = pl.BlockSpec((tm, d), lambda i, f: (i, 0))
    mod_l = pl.BlockSpec((None, N_MOD, d), lambda i, f: (i // tpb, 0, 0))
    mod_c = pl.BlockSpec((None, N_MOD, d), lambda i, f: (batch, 0, 0))
    in_specs = [row, row, mod_l, mod_c,
                pl.BlockSpec((d, tf), lambda i, f: (0, f)),
                pl.BlockSpec((tf, d), lambda i, f: (f, 0))]
    args = [h, x, mods, mods, w1, w2]
    out_specs, out_shape = row, jax.ShapeDtypeStruct((t, d), F32)
    if emit_next:
        in_specs += [mod_l, mod_c, pl.BlockSpec((1, d), lambda i, f: (0, 0))]
        args += [next_mods, next_mods, next_g]
        out_specs, out_shape = [row, row], [out_shape, jax.ShapeDtypeStruct((t, d), BF16)]
    return pl.pallas_call(
        functools.partial(_mlp_kernel, tm=tm, tpb=tpb, n_lat=n_lat, emit_next=emit_next),
        grid=(t // tm, ff // tf),
        in_specs=in_specs,
        out_specs=out_specs,
        out_shape=out_shape,
        scratch_shapes=[pltpu.VMEM((tm, d), F32)],
        compiler_params=_params(("parallel", "arbitrary"), VMEM_LIMIT),
        name="sq_relu_mlp",
    )(*args)


def _final_norm_kernel(x_ref, g_ref, o_ref):
    o_ref[...] = _rms(x_ref[...], g_ref[...])


def _final_norm_call(x, g, n_tok, n_lat, batch):
    d = x.shape[1]
    tr = _pick(math.gcd(n_lat, n_tok), (256, 128))
    lt = n_lat // tr
    tpb = n_tok // tr
    return pl.pallas_call(
        _final_norm_kernel,
        grid=(batch, lt),
        in_specs=[pl.BlockSpec((tr, d), lambda b, i: (b * tpb + i, 0)),
                  pl.BlockSpec((1, d), lambda b, i: (0, 0))],
        out_specs=pl.BlockSpec((None, tr, d), lambda b, i: (b, i, 0)),
        out_shape=jax.ShapeDtypeStruct((batch, n_lat, d), F32),
        compiler_params=_params(("parallel", "parallel")),
        name="final_norm",
    )(x, g)


def _rope_tables(n_lat, n_ctx):
    rows = n_lat // GRID_W
    row = jnp.broadcast_to(jnp.arange(rows)[:, None], (rows, GRID_W)).reshape(-1).astype(F32)
    col = jnp.broadcast_to(jnp.arange(GRID_W)[None, :], (rows, GRID_W)).reshape(-1).astype(F32)
    inv = jnp.float32(ROPE_BASE) ** (-jnp.arange(ROPE_PAIRS, dtype=F32) / ROPE_PAIRS)
    ang = jnp.concatenate([row[:, None] * inv, col[:, None] * inv], axis=-1)
    cos, sin = jnp.cos(ang), jnp.sin(ang)
    cos_t = jnp.concatenate([cos, cos, cos, cos], axis=-1)
    sin_t = jnp.concatenate([-sin, sin, -sin, sin], axis=-1)
    cos_t = jnp.concatenate([cos_t, jnp.ones((n_ctx, ATT_BLOCK), F32)], axis=0)
    sin_t = jnp.concatenate([sin_t, jnp.zeros((n_ctx, ATT_BLOCK), F32)], axis=0)
    return cos_t, sin_t


def kernel(x, c, ctx, c_ctx, ada_w, ada_b, norm_mix_g, w_in, conv_w, conv_b, dt_bias_f, dt_bias_b, a_log_f, a_log_b, ssd_d, ssd_norm_g, lambda_q1, lambda_k1, lambda_q2, lambda_k2, attn_subln_g, w_attn_o, w_ssd_o, w_out, norm_mlp_g, w_mlp1, w_mlp2, final_norm_g):
    batch, n_lat, d = x.shape
    n_ctx = ctx.shape[1]
    n_tok = n_lat + n_ctx
    depth = ada_w.shape[0]
    gh = SSD_GROUP_HEADS

    o_q, o_k, o_v = 0, 1024, 2048
    o_z = 3072
    o_xbc = o_z + SSD_D_INNER
    o_dt = o_xbc + SSD_D_INNER + SSD_BC
    o_ga = o_dt + 2 * SSD_HEADS
    o_gs = o_ga + D_MODEL

    scale = ATT_HEAD_DIM ** -0.5 * math.log2(math.e)
    w_qk = jnp.concatenate([w_in[:, :, o_q:o_k] * scale, w_in[:, :, o_k:o_v]], axis=-1).astype(BF16)
    w_zvg = jnp.concatenate([w_in[:, :, o_z:o_xbc], w_in[:, :, o_v:o_z], w_in[:, :, o_ga:]], axis=-1).astype(BF16)
    z_col0, v_col0, ga_col0, gs_col0 = 0, SSD_D_INNER, SSD_D_INNER + D_MODEL, SSD_D_INNER + 2 * D_MODEL
    w_xs = w_in[:, :, o_xbc:o_xbc + SSD_D_INNER].astype(BF16)
    w_bcm = w_in[:, :, o_xbc + SSD_D_INNER:o_dt].astype(BF16)
    perm = jnp.asarray([f * SSD_HEADS + g * gh + h for g in range(SSD_GROUPS) for f in range(2) for h in range(gh)])
    w_dt_t = jnp.swapaxes(w_in[:, :, o_dt:o_ga][:, :, perm], 1, 2).astype(BF16)
    dt_bias = jnp.concatenate([dt_bias_f, dt_bias_b], axis=-1)[:, perm][:, :, None]
    a_log = jnp.concatenate([a_log_f, a_log_b], axis=-1)[:, perm].reshape(depth, SSD_GROUPS, 2 * gh, 1)
    a_log = jnp.broadcast_to(a_log, (depth, SSD_GROUPS, 2 * gh, SSD_CHUNK)).astype(F32)
    d_x = jnp.repeat(ssd_d, SSD_HEAD_DIM, axis=-1)[:, None, :].astype(F32)
    wa_b, ws_b, wo_b = w_attn_o.astype(BF16), w_ssd_o.astype(BF16), w_out.astype(BF16)
    w1_b, w2_b = w_mlp1.astype(BF16), w_mlp2.astype(BF16)
    lam_vecs = jnp.stack([lambda_q1, lambda_k1, lambda_q2, lambda_k2], axis=1).astype(F32)
    rope = _rope_tables(n_lat, n_ctx)

    rows_pad = -(-(batch + 1) // MOD_ROWS_PAD) * MOD_ROWS_PAD
    c_all = jnp.concatenate([c, c_ctx[None, :], jnp.zeros((rows_pad - batch - 1, d), F32)], axis=0)
    mods = _mod_call(c_all, ada_w, ada_b).reshape(depth, rows_pad, N_MOD, d)

    xt, h = _embed_call(x, ctx, mods[0], norm_mix_g[0][None, :])
    for l in range(depth):
        lam_init = 0.8 - 0.6 * math.exp(-0.3 * l)
        qk = _proj_call(h, w_qk[l], BF16, n_tok, rope=rope)
        zvg = _proj_call(h, w_zvg[l], BF16, n_tok)
        dt = _dt_call(h, w_dt_t[l], dt_bias[l], n_tok, batch)
        dt = dt.reshape(batch, SSD_GROUPS, 2 * gh, n_tok)
        xs = _proj_conv_call(h, w_xs[l], conv_w[l, :, :SSD_D_INNER], conv_b[l, None, :SSD_D_INNER], F32, n_tok, n_lat)
        bcm = _proj_conv_call(h, w_bcm[l], conv_w[l, :, SSD_D_INNER:], conv_b[l, None, SSD_D_INNER:], BF16, n_tok, n_lat)
        att = _attn_call(qk, zvg, v_col0, lam_vecs[l], attn_subln_g[l][None, :], lam_init, n_tok, n_lat, batch)
        y = _ssd_call(xs, bcm, dt, a_log[l], d_x[l], n_tok, n_lat, batch)
        xt, h2 = _mix_out_call(att, y, zvg, z_col0, ga_col0, gs_col0, xt, mods[l], ssd_norm_g[l][None, :],
                               norm_mlp_g[l][None, :], wa_b[l], ws_b[l], wo_b[l], n_tok, n_lat, batch)
        if l + 1 < depth:
            xt, h = _mlp_call(h2, xt, mods[l], w1_b[l], w2_b[l], n_tok, n_lat, batch,
                              next_mods=mods[l + 1], next_g=norm_mix_g[l + 1][None, :])
        else:
            xt = _mlp_call(h2, xt, mods[l], w1_b[l], w2_b[l], n_tok, n_lat, batch)
    return _final_norm_call(xt, final_norm_g[None, :], n_tok, n_lat, batch)
```
